```python
import math
import jax, jax.numpy as jnp
from jax import lax
import numpy as np

D_MODEL = 1024
BATCH = 8
SEQ = 4096
DEPTH = 2

HEAD_DIM = 128
N_A_LAYERS = DEPTH // 2
N_B_LAYERS = DEPTH - N_A_LAYERS
MAIN_WIDTH = D_MODEL
POOL_WINDOWS = (2, 4, 8, 16)
POOL_GROUP = MAIN_WIDTH // len(POOL_WINDOWS)
MOBA_HEADS = MAIN_WIDTH // HEAD_DIM
MOBA_BLOCK = 256
MOBA_TOPK = 3
MOBA_CHUNK = 8
MEM_TOKENS = 256
MEM_HEADS = 4
MEM_WIDTH = MEM_HEADS * HEAD_DIM
MIX_WIDTH = MAIN_WIDTH + MEM_WIDTH
ROPE_THETA = 10000.0
PEER_HEADS = 8
PEER_NKEYS = 128
PEER_EXPERTS = PEER_NKEYS * PEER_NKEYS
PEER_TOPK = 16
PEER_QDIM = 256
PEER_HALF = PEER_QDIM // 2
PEER_CHUNK = 128
RMS_EPS = 1e-6
NEG = -1e30

kernel_name = "yoco_pool_moba_peer_hybrid"


def rmsnorm(x, g):
    xf = x.astype(jnp.float32)
    y = xf * lax.rsqrt(jnp.mean(xf * xf, axis=-1, keepdims=True) + RMS_EPS)
    return (y * g.astype(jnp.float32)).astype(x.dtype)


def rope_tables(S):
    inv = 1.0 / (ROPE_THETA ** (jnp.arange(0, HEAD_DIM, 2, dtype=jnp.float32) / HEAD_DIM))
    ang = jnp.arange(S, dtype=jnp.float32)[:, None] * inv[None, :]
    return jnp.cos(ang), jnp.sin(ang)


def apply_rope(x, cos, sin):
    xf = x.astype(jnp.float32)
    x1, x2 = xf[..., : HEAD_DIM // 2], xf[..., HEAD_DIM // 2:]
    return jnp.concatenate([x1 * cos - x2 * sin, x2 * cos + x1 * sin], axis=-1).astype(x.dtype)


def pool_mixer(main, pool_w, pool_scale):
    B, S, _ = main.shape
    t = jnp.arange(S)
    outs = []
    for g, w in enumerate(POOL_WINDOWS):
        xg = main[..., g * POOL_GROUP:(g + 1) * POOL_GROUP].astype(jnp.float32)
        c = lax.cumsum(xg, axis=1)
        lag = jnp.pad(c, ((0, 0), (w, 0), (0, 0)))[:, :S]
        cnt = jnp.minimum(t + 1, w).astype(jnp.float32)[None, :, None]
        d = ((c - lag) / cnt - xg).astype(main.dtype)
        outs.append(jnp.einsum('bsc,ce->bse', d, pool_w[g]))
    return jnp.concatenate(outs, axis=-1) * pool_scale


def memory_attention(qm, mem, w_mem_kv):
    B, S, _ = qm.shape
    M = mem.shape[1]
    kv = mem @ w_mem_kv
    k = kv[..., :MEM_WIDTH].reshape(B, M, MEM_HEADS, HEAD_DIM)
    v = kv[..., MEM_WIDTH:].reshape(B, M, MEM_HEADS, HEAD_DIM)
    q = qm.reshape(B, S, MEM_HEADS, HEAD_DIM)
    logits = jnp.einsum('bshd,bmhd->bhsm', q, k).astype(jnp.float32) * (HEAD_DIM ** -0.5)
    p = jax.nn.softmax(logits, axis=-1).astype(v.dtype)
    return jnp.einsum('bhsm,bmhd->bshd', p, v).reshape(B, S, MEM_WIDTH)


def shared_kv(x, kv_norm_g, w_kv_shared, cos, sin):
    B, S, _ = x.shape
    h = rmsnorm(x, kv_norm_g)
    kv = h @ w_kv_shared
    k = kv[..., :MAIN_WIDTH].reshape(B, S, MOBA_HEADS, HEAD_DIM).transpose(0, 2, 1, 3)
    v = kv[..., MAIN_WIDTH:].reshape(B, S, MOBA_HEADS, HEAD_DIM).transpose(0, 2, 1, 3)
    k = apply_rope(k, cos, sin)
    nb = -(-S // MOBA_BLOCK)
    pad = nb * MOBA_BLOCK - S
    k = jnp.pad(k, ((0, 0), (0, 0), (0, pad), (0, 0))).reshape(B, MOBA_HEADS, nb, MOBA_BLOCK, HEAD_DIM)
    v = jnp.pad(v, ((0, 0), (0, 0), (0, pad), (0, 0))).reshape(B, MOBA_HEADS, nb, MOBA_BLOCK, HEAD_DIM)
    kmean = jnp.mean(k.astype(jnp.float32), axis=3)
    return k, v, kmean


def moba_attention(qmain, k_blocks, v_blocks, kmean, cos, sin):
    B, S, _ = qmain.shape
    H, hd, BLK = MOBA_HEADS, HEAD_DIM, MOBA_BLOCK
    nb = k_blocks.shape[2]
    topk = min(MOBA_TOPK, nb)
    q = apply_rope(qmain.reshape(B, S, H, hd).transpose(0, 2, 1, 3), cos, sin)
    t = jnp.arange(S)
    own = t // BLK
    gate = jnp.einsum('bhsd,bhnd->bhsn', q.astype(jnp.float32), kmean)
    past = jnp.arange(nb)[None, :] < own[:, None]
    gate = jnp.where(past, gate, NEG)
    _, sel = lax.top_k(gate, topk)
    valid = sel < own[:, None]
    nc = S // MOBA_CHUNK

    def to_chunks(a):
        a = a.reshape(a.shape[:2] + (nc, MOBA_CHUNK) + a.shape[3:])
        return jnp.moveaxis(a, 2, 0)

    scale = hd ** -0.5
    bi = jnp.arange(B)[:, None, None, None]
    hi = jnp.arange(H)[None, :, None, None]
    kpos_local = jnp.arange(BLK)

    def attend(args):
        q_c, sel_c, valid_c, pos_c = args
        kg = k_blocks[bi, hi, sel_c]
        vg = v_blocks[bi, hi, sel_c]
        blk = pos_c[0] // BLK
        k_own = lax.dynamic_index_in_dim(k_blocks, blk, axis=2, keepdims=False)
        v_own = lax.dynamic_index_in_dim(v_blocks, blk, axis=2, keepdims=False)
        s_sel = jnp.einsum('bhcd,bhcnkd->bhcnk', q_c, kg).astype(jnp.float32) * scale
        s_sel = jnp.where(valid_c[..., None], s_sel, NEG).reshape(B, H, MOBA_CHUNK, topk * BLK)
        s_own = jnp.einsum('bhcd,bhkd->bhck', q_c, k_own).astype(jnp.float32) * scale
        kpos = blk * BLK + kpos_local
        s_own = jnp.where(kpos[None, :] <= pos_c[:, None], s_own, NEG)
        p = jax.nn.softmax(jnp.concatenate([s_sel, s_own], axis=-1), axis=-1).astype(v_blocks.dtype)
        p_sel = p[..., : topk * BLK].reshape(B, H, MOBA_CHUNK, topk, BLK)
        p_own = p[..., topk * BLK:]
        return (jnp.einsum('bhcnk,bhcnkd->bhcd', p_sel, vg)
                + jnp.einsum('bhck,bhkd->bhcd', p_own, v_own))

    out = lax.map(attend, (to_chunks(q), to_chunks(sel), to_chunks(valid), t.reshape(nc, MOBA_CHUNK)))
    out = jnp.moveaxis(out, 0, 2).reshape(B, H, S, hd)
    return out.transpose(0, 2, 1, 3).reshape(B, S, H * hd)


def peer_ffn(h, wq, subkeys, u, v):
    B, S, D = h.shape
    T = B * S
    ht = h.reshape(T, D)
    q = (ht @ wq).reshape(T, PEER_HEADS, 2, PEER_HALF).astype(jnp.float32)
    s = jnp.einsum('thpd,hpnd->thpn', q, subkeys.astype(jnp.float32))
    sv, si = lax.top_k(s, PEER_TOPK)
    cand = (sv[:, :, 0, :, None] + sv[:, :, 1, None, :]).reshape(T, PEER_HEADS, PEER_TOPK * PEER_TOPK)
    cv, ci = lax.top_k(cand, PEER_TOPK)
    e1 = jnp.take_along_axis(si[:, :, 0], ci // PEER_TOPK, axis=-1)
    e2 = jnp.take_along_axis(si[:, :, 1], ci % PEER_TOPK, axis=-1)
    experts = e1 * PEER_NKEYS + e2
    gates = jax.nn.softmax(cv, axis=-1)
    nc = T // PEER_CHUNK
    E = PEER_HEADS * PEER_TOPK

    def expert_chunk(args):
        xc, ec, gc = args
        a = jnp.einsum('cd,ced->ce', xc, u[ec]).astype(jnp.float32)
        hh = (jax.nn.gelu(a, approximate=False) * gc).astype(v.dtype)
        return jnp.einsum('ce,ced->cd', hh, v[ec])

    out = lax.map(expert_chunk, (ht.reshape(nc, PEER_CHUNK, D),
                                 experts.reshape(nc, PEER_CHUNK, E),
                                 gates.reshape(nc, PEER_CHUNK, E)))
    return out.reshape(B, S, D)


def setup_inputs(seed: int = 0) -> dict:
    key = jax.random.key(seed)
    ks = jax.random.split(key, 20)
    f32 = jnp.float32
    D = D_MODEL

    def nrm(k, shape, scale):
        return jax.random.normal(k, shape, f32) * scale

    return {
        "x": nrm(ks[0], (BATCH, SEQ, D), 1.0),
        "mem": nrm(ks[1], (BATCH, MEM_TOKENS, D), 1.0),
        "norm_mix_g": 1.0 + nrm(ks[2], (DEPTH, D), 0.05),
        "w_in": nrm(ks[3], (DEPTH, D, MIX_WIDTH), D ** -0.5),
        "w_out": nrm(ks[4], (DEPTH, MIX_WIDTH, D), MIX_WIDTH ** -0.5),
        "w_mem_kv": nrm(ks[5], (DEPTH, D, 2 * MEM_WIDTH), D ** -0.5),
        "pool_w": nrm(ks[6], (N_A_LAYERS, len(POOL_WINDOWS), POOL_GROUP, POOL_GROUP), POOL_GROUP ** -0.5),
        "pool_scale": 1.0 + nrm(ks[7], (N_A_LAYERS, MAIN_WIDTH), 0.05),
        "kv_norm_g": 1.0 + nrm(ks[8], (D,), 0.05),
        "w_kv_shared": nrm(ks[9], (D, 2 * MAIN_WIDTH), D ** -0.5),
        "norm_ffn_g": 1.0 + nrm(ks[10], (DEPTH, D), 0.05),
        "peer_wq": nrm(ks[11], (DEPTH, D, PEER_HEADS * PEER_QDIM), D ** -0.5),
        "peer_subkeys": nrm(ks[12], (DEPTH, PEER_HEADS, 2, PEER_NKEYS, PEER_HALF), PEER_HALF ** -0.5),
        "peer_u": nrm(ks[13], (DEPTH, PEER_EXPERTS, D), D ** -0.5),
        "peer_v": nrm(ks[14], (DEPTH, PEER_EXPERTS, D), PEER_HEADS ** -0.5),
        "final_norm_g": 1.0 + nrm(ks[15], (D,), 0.05),
    }


def reference(x, mem, norm_mix_g, w_in, w_out, w_mem_kv, pool_w, pool_scale, kv_norm_g,
              w_kv_shared, norm_ffn_g, peer_wq, peer_subkeys, peer_u, peer_v, final_norm_g):
    S = x.shape[1]
    cos, sin = rope_tables(S)
    shared = None
    for l in range(DEPTH):
        h = rmsnorm(x, norm_mix_g[l])
        proj = h @ w_in[l]
        main, qm = proj[..., :MAIN_WIDTH], proj[..., MAIN_WIDTH:]
        if l < N_A_LAYERS:
            mixed = pool_mixer(main, pool_w[l], pool_scale[l])
        else:
            k_blocks, v_blocks, kmean = shared
            mixed = moba_attention(main, k_blocks, v_blocks, kmean, cos, sin)
        mem_out = memory_attention(qm, mem, w_mem_kv[l])
        x = x + jnp.concatenate([mixed, mem_out], axis=-1) @ w_out[l]
        x = x + peer_ffn(rmsnorm(x, norm_ffn_g[l]), peer_wq[l], peer_subkeys[l], peer_u[l], peer_v[l])
        if l == N_A_LAYERS - 1:
            shared = shared_kv(x, kv_norm_g, w_kv_shared, cos, sin)
    return rmsnorm(x, final_norm_g)
```

```python
import functools
import math

import jax
import jax.numpy as jnp
from jax import lax
from jax.experimental import pallas as pl
from jax.experimental.pallas import tpu as pltpu

F32 = jnp.float32
BF16 = jnp.bfloat16

D_MODEL = 1024
HEAD_DIM = 128
HALF_HEAD = HEAD_DIM // 2
MAIN_WIDTH = 1024
POOL_WINDOWS = (2, 4, 8, 16)
POOL_GROUP = MAIN_WIDTH // len(POOL_WINDOWS)
MOBA_HEADS = MAIN_WIDTH // HEAD_DIM
MOBA_BLOCK = 256
MOBA_TOPK = 3
MEM_TOKENS = 256
MEM_HEADS = 4
MEM_WIDTH = MEM_HEADS * HEAD_DIM
MIX_WIDTH = MAIN_WIDTH + MEM_WIDTH
ROPE_THETA = 10000.0
PEER_HEADS = 8
PEER_NKEYS = 128
PEER_EXPERTS = PEER_NKEYS * PEER_NKEYS
PEER_TOPK = 16
PEER_QDIM = 256
RMS_EPS = 1e-6
NEG = -1e30
ATTN_SCALE = HEAD_DIM ** -0.5

LANES = 128
TOKEN_TILE = 512
LANE_CHUNKS = TOKEN_TILE // LANES
EXPERT_TILE = 1024
EXPERT_ROWS = EXPERT_TILE // PEER_NKEYS
VMEM_LIMIT = 56 * 1024 * 1024
STAIR_COUNTS = tuple(PEER_TOPK // (a + 1) for a in range(PEER_TOPK))
STAIR_ROWS = sum(STAIR_COUNTS)
STAIR_PAD = -(-STAIR_ROWS // 8) * 8
RANK_NONE = 64.0


def _dot(a, b):
    return jnp.dot(a, b, preferred_element_type=F32)


def _dot_hi(a, b):
    return jnp.dot(a, b, preferred_element_type=F32, precision=lax.Precision.HIGHEST)


def _rmsnorm_t(x, g_lanes):
    ms = jnp.sum(x * x, axis=0, keepdims=True) * (1.0 / x.shape[0])
    y = x * lax.rsqrt(ms + RMS_EPS)
    chunks = [y[:, c * LANES:(c + 1) * LANES] * g_lanes for c in range(x.shape[1] // LANES)]
    return jnp.concatenate(chunks, axis=1)


def _scale_rows(x, s_lanes):
    chunks = [x[:, c * LANES:(c + 1) * LANES] * s_lanes for c in range(x.shape[1] // LANES)]
    return jnp.concatenate(chunks, axis=1)


def _rope_t(x, cos, sin):
    outs = []
    for h in range(x.shape[0] // HEAD_DIM):
        x1 = x[h * HEAD_DIM:h * HEAD_DIM + HALF_HEAD]
        x2 = x[h * HEAD_DIM + HALF_HEAD:(h + 1) * HEAD_DIM]
        outs.append(x1 * cos - x2 * sin)
        outs.append(x2 * cos + x1 * sin)
    return jnp.concatenate(outs, axis=0)


def _memory_attention_t(qm, mem_k_ref, mem_vt_ref):
    outs = []
    for h in range(MEM_HEADS):
        rows = slice(h * HEAD_DIM, (h + 1) * HEAD_DIM)
        q = qm[rows].astype(BF16)
        logits = _dot(mem_k_ref[:, rows], q) * ATTN_SCALE
        m = jnp.max(logits, axis=0, keepdims=True)
        e = jnp.exp(logits - m)
        z = jnp.sum(e, axis=0, keepdims=True)
        o = _dot(mem_vt_ref[rows, :], e.astype(BF16))
        outs.append(o * (1.0 / z))
    return jnp.concatenate(outs, axis=0)


def _mem_kv_kernel(mem_ref, memt_ref, wk_ref, wvt_ref, k_ref, vt_ref):
    k_ref[...] = _dot(mem_ref[...], wk_ref[...]).astype(BF16)
    vt_ref[...] = _dot(wvt_ref[...], memt_ref[...]).astype(BF16)


def _mem_kv(mem_bf, memt_bf, wk, wvt):
    n_layers, batch = wk.shape[0], mem_bf.shape[0]
    return pl.pallas_call(
        _mem_kv_kernel,
        grid=(n_layers, batch),
        in_specs=[
            pl.BlockSpec((None, MEM_TOKENS, D_MODEL), lambda l, b: (b, 0, 0)),
            pl.BlockSpec((None, D_MODEL, MEM_TOKENS), lambda l, b: (b, 0, 0)),
            pl.BlockSpec((None, D_MODEL, MEM_WIDTH), lambda l, b: (l, 0, 0)),
            pl.BlockSpec((None, MEM_WIDTH, D_MODEL), lambda l, b: (l, 0, 0)),
        ],
        out_specs=[
            pl.BlockSpec((None, None, MEM_TOKENS, MEM_WIDTH), lambda l, b: (l, b, 0, 0)),
            pl.BlockSpec((None, None, MEM_WIDTH, MEM_TOKENS), lambda l, b: (l, b, 0, 0)),
        ],
        out_shape=[
            jax.ShapeDtypeStruct((n_layers, batch, MEM_TOKENS, MEM_WIDTH), BF16),
            jax.ShapeDtypeStruct((n_layers, batch, MEM_WIDTH, MEM_TOKENS), BF16),
        ],
        name="mem_kv",
    )(mem_bf, memt_bf, wk, wvt)


def _pool_layer_kernel(x_ref, g_ref, win_ref, poolw_ref, pscale_ref, memk_ref, memvt_ref,
                       wout_ref, o_ref, halo_ref):
    s = pl.program_id(1)

    @pl.when(s == 0)
    def _():
        halo_ref[...] = jnp.zeros_like(halo_ref)

    x = x_ref[...]
    h = _rmsnorm_t(x, g_ref[...]).astype(BF16)
    proj = _dot(win_ref[...], h)
    main = proj[:MAIN_WIDTH]
    qm = proj[MAIN_WIDTH:]

    ext = jnp.concatenate([halo_ref[...], main], axis=1)
    halo_ref[...] = main[:, TOKEN_TILE - LANES:]
    tpos = s * TOKEN_TILE + lax.broadcasted_iota(jnp.int32, (1, TOKEN_TILE), 1)

    mixed = []
    win = ext
    for g, w in enumerate(POOL_WINDOWS):
        win = win[(POOL_GROUP if g else 0):]
        win = win + pltpu.roll(win, w // 2, axis=1)
        inv_cnt = 1.0 / jnp.minimum(tpos + 1, w).astype(F32)
        grp = slice(g * POOL_GROUP, (g + 1) * POOL_GROUP)
        d = win[:POOL_GROUP, LANES:] * inv_cnt - main[grp]
        mixed.append(_dot(poolw_ref[g], d.astype(BF16)))
    mixed = _scale_rows(jnp.concatenate(mixed, axis=0), pscale_ref[...])

    mem_out = _memory_attention_t(qm, memk_ref, memvt_ref)
    cat = jnp.concatenate([mixed, mem_out], axis=0).astype(BF16)
    o_ref[...] = x + _dot(wout_ref[...], cat)


def _pool_layer(xt, g_lanes, win_t, poolw_t, pscale_lanes, mem_k, mem_vt, wout_t):
    batch, _, seq = xt.shape
    tile = pl.BlockSpec((None, D_MODEL, TOKEN_TILE), lambda b, s: (b, 0, s))
    full = lambda shape: pl.BlockSpec(shape, lambda b, s: (0,) * len(shape))
    return pl.pallas_call(
        _pool_layer_kernel,
        grid=(batch, seq // TOKEN_TILE),
        in_specs=[
            tile,
            full((D_MODEL, LANES)),
            full((MIX_WIDTH, D_MODEL)),
            full((len(POOL_WINDOWS), POOL_GROUP, POOL_GROUP)),
            full((MAIN_WIDTH, LANES)),
            pl.BlockSpec((None, MEM_TOKENS, MEM_WIDTH), lambda b, s: (b, 0, 0)),
            pl.BlockSpec((None, MEM_WIDTH, MEM_TOKENS), lambda b, s: (b, 0, 0)),
            full((D_MODEL, MIX_WIDTH)),
        ],
        out_specs=tile,
        out_shape=jax.ShapeDtypeStruct(xt.shape, F32),
        scratch_shapes=[pltpu.VMEM((MAIN_WIDTH, LANES), F32)],
        compiler_params=pltpu.CompilerParams(
            dimension_semantics=("arbitrary", "arbitrary"), vmem_limit_bytes=VMEM_LIMIT),
        name="pool_layer",
    )(xt, g_lanes, win_t, poolw_t, pscale_lanes, mem_k, mem_vt, wout_t)


def _top16(s, track_rank):
    work = s
    rank = jnp.full(s.shape, RANK_NONE, F32)
    rows = []
    for k in range(PEER_TOPK):
        m = jnp.max(work, axis=0, keepdims=True)
        rows.append(m)
        hit = work == m
        if track_rank:
            rank = jnp.where(hit, float(k), rank)
        if k + 1 < PEER_TOPK:
            work = jnp.where(hit, -jnp.inf, work)
    return rows, rank


def _peer_select_chunk(s1, s2, sv2_ref, e2_ref, sums_ref, wts_ref):
    sv1, _ = _top16(s1, False)
    sv2, rank2 = _top16(s2, True)
    for k in range(PEER_TOPK):
        sv2_ref[k:k + 1, :] = sv2[k]
        e2_ref[k:k + 1, :] = jnp.exp(sv2[k] - sv2[0])
    off = 0
    for a, nb in enumerate(STAIR_COUNTS):
        sums_ref[off:off + nb, :] = sv2_ref[0:nb, :] + sv1[a]
        wts_ref[off:off + nb, :] = e2_ref[0:nb, :] * jnp.exp(sv1[a] - sv1[0])
        off += nb
    if STAIR_PAD > STAIR_ROWS:
        pad = (STAIR_PAD - STAIR_ROWS, LANES)
        sums_ref[STAIR_ROWS:STAIR_PAD, :] = jnp.full(pad, -jnp.inf, F32)
        wts_ref[STAIR_ROWS:STAIR_PAD, :] = jnp.zeros(pad, F32)

    sums = sums_ref[...]
    work = sums
    seen = jnp.zeros((1, LANES), F32)
    tau = jnp.full((1, LANES), -jnp.inf, F32)
    for k in range(PEER_TOPK):
        m = jnp.max(work, axis=0, keepdims=True)
        hit = work == m
        tau = jnp.where(seen < PEER_TOPK, m, tau)
        seen = seen + jnp.sum(jnp.where(hit, 1.0, 0.0), axis=0, keepdims=True)
        work = jnp.where(hit, -jnp.inf, work)
    z = jnp.sum(jnp.where(sums >= tau, wts_ref[...], 0.0), axis=0, keepdims=True)

    cnt = jnp.zeros(s1.shape, F32)
    for b in range(PEER_TOPK):
        cnt = cnt + jnp.where(s1 + sv2[b] >= tau, 1.0, 0.0)
    p1 = jnp.exp(s1 - sv1[0])
    r2 = jnp.exp(s2 - sv2[0]) * (1.0 / z)
    return p1, cnt, rank2, r2


def _gelu(a):
    return 0.5 * a * (1.0 + lax.erf(a * (1.0 / math.sqrt(2.0))))


def _peer_kernel(x_ref, g_ref, wq_ref, subk_ref, u_ref, vt_ref, gfin_ref, o_ref,
                 hn_ref, acc_ref, s_ref, p1_ref, cnt_ref, rank2_ref, r2_ref,
                 sv2_ref, e2_ref, sums_ref, wts_ref, a_ref, h_ref, *, final_norm):
    e = pl.program_id(2)

    @pl.when(e == 0)
    def _():
        hn_ref[...] = _rmsnorm_t(x_ref[...], g_ref[...]).astype(BF16)
        acc_ref[...] = jnp.zeros_like(acc_ref)

        def head_body(h, carry):
            row0 = pl.multiple_of(h * PEER_QDIM, PEER_QDIM)
            q = _dot(wq_ref[pl.ds(row0, PEER_QDIM), :], hn_ref[...])
            s_ref[0] = _dot_hi(subk_ref[h, 0], q[:PEER_NKEYS])
            s_ref[1] = _dot_hi(subk_ref[h, 1], q[PEER_NKEYS:])
            for c in range(LANE_CHUNKS):
                lanes = slice(c * LANES, (c + 1) * LANES)
                p1, cnt, rank2, r2 = _peer_select_chunk(
                    s_ref[0, :, lanes], s_ref[1, :, lanes], sv2_ref, e2_ref, sums_ref, wts_ref)
                p1_ref[h, :, lanes] = p1
                cnt_ref[h, :, lanes] = cnt
                rank2_ref[h, :, lanes] = rank2
                r2_ref[h, :, lanes] = r2
            return carry

        lax.fori_loop(0, PEER_HEADS, head_body, 0)

    a_ref[...] = _dot(u_ref[...], hn_ref[...])

    i0 = pl.ds(pl.multiple_of(e * EXPERT_ROWS, EXPERT_ROWS), EXPERT_ROWS)
    for c in range(LANE_CHUNKS):
        lanes = slice(c * LANES, (c + 1) * LANES)
        cnt_rows = [cnt_ref[h, i0, lanes] for h in range(PEER_HEADS)]
        p1_rows = [p1_ref[h, i0, lanes] for h in range(PEER_HEADS)]
        for r in range(EXPERT_ROWS):
            rows = slice(r * PEER_NKEYS, (r + 1) * PEER_NKEYS)
            gate = jnp.zeros((PEER_NKEYS, LANES), F32)
            for h in range(PEER_HEADS):
                picked = jnp.where(rank2_ref[h, :, lanes] < cnt_rows[h][r:r + 1],
                                   r2_ref[h, :, lanes], 0.0)
                gate = gate + picked * p1_rows[h][r:r + 1]
            h_ref[rows, lanes] = (_gelu(a_ref[rows, lanes]) * gate).astype(BF16)
    acc_ref[...] += _dot(vt_ref[...], h_ref[...])

    @pl.when(e == pl.num_programs(2) - 1)
    def _():
        y = x_ref[...] + acc_ref[...]
        if final_norm:
            y = _rmsnorm_t(y, gfin_ref[...])
        o_ref[...] = y


def _peer_layer(xt, g_lanes, wq_t, subkeys, u_bf, vt_bf, gfin_lanes, final_norm):
    batch, _, seq = xt.shape
    n_steps = PEER_EXPERTS // EXPERT_TILE
    tile = pl.BlockSpec((None, D_MODEL, TOKEN_TILE), lambda b, s, e: (b, 0, s))
    full = lambda shape: pl.BlockSpec(shape, lambda b, s, e: (0,) * len(shape))
    sel = lambda: pltpu.VMEM((PEER_HEADS, PEER_NKEYS, TOKEN_TILE), F32)
    return pl.pallas_call(
        functools.partial(_peer_kernel, final_norm=final_norm),
        grid=(batch, seq // TOKEN_TILE, n_steps),
        in_specs=[
            tile,
            full((D_MODEL, LANES)),
            full((PEER_HEADS * PEER_QDIM, D_MODEL)),
            full((PEER_HEADS, 2, PEER_NKEYS, PEER_NKEYS)),
            pl.BlockSpec((EXPERT_TILE, D_MODEL), lambda b, s, e: (e, 0)),
            pl.BlockSpec((D_MODEL, EXPERT_TILE), lambda b, s, e: (0, e)),
            full((D_MODEL, LANES)),
        ],
        out_specs=tile,
        out_shape=jax.ShapeDtypeStruct(xt.shape, F32),
        scratch_shapes=[
            pltpu.VMEM((D_MODEL, TOKEN_TILE), BF16),
            pltpu.VMEM((D_MODEL, TOKEN_TILE), F32),
            pltpu.VMEM((2, PEER_NKEYS, TOKEN_TILE), F32),
            sel(), sel(), sel(), sel(),
            pltpu.VMEM((PEER_TOPK, LANES), F32),
            pltpu.VMEM((PEER_TOPK, LANES), F32),
            pltpu.VMEM((STAIR_PAD, LANES), F32),
            pltpu.VMEM((STAIR_PAD, LANES), F32),
            pltpu.VMEM((EXPERT_TILE, TOKEN_TILE), F32),
            pltpu.VMEM((EXPERT_TILE, TOKEN_TILE), BF16),
        ],
        compiler_params=pltpu.CompilerParams(
            dimension_semantics=("arbitrary", "arbitrary", "arbitrary"),
            vmem_limit_bytes=VMEM_LIMIT),
        name="peer_final" if final_norm else "peer",
    )(xt, g_lanes, wq_t, subkeys, u_bf, vt_bf, gfin_lanes)


def _shared_kv_kernel(x_ref, g_ref, wkv_ref, cos_ref, sin_ref, k_ref, vt_ref, kmean_ref):
    h = _rmsnorm_t(x_ref[...], g_ref[...]).astype(BF16)
    kv = _dot(wkv_ref[...], h)
    k_t = _rope_t(kv[:MAIN_WIDTH], cos_ref[...], sin_ref[...])
    v_t = kv[MAIN_WIDTH:]
    k = k_t.T
    for jb in range(TOKEN_TILE // MOBA_BLOCK):
        toks = slice(jb * MOBA_BLOCK, (jb + 1) * MOBA_BLOCK)
        for hh in range(MOBA_HEADS):
            cols = slice(hh * HEAD_DIM, (hh + 1) * HEAD_DIM)
            kb = k[toks, cols]
            k_ref[hh, jb] = kb.astype(BF16)
            vt_ref[hh, jb] = v_t[cols, toks].astype(BF16)
            kmean_ref[hh, jb:jb + 1, :] = jnp.sum(kb, axis=0, keepdims=True) * (1.0 / MOBA_BLOCK)


def _shared_kv(xt, g_lanes, wkv_t, cos_t, sin_t):
    batch, _, seq = xt.shape
    nb = seq // MOBA_BLOCK
    per_tile = TOKEN_TILE // MOBA_BLOCK
    full = lambda shape: pl.BlockSpec(shape, lambda b, s: (0,) * len(shape))
    outs = pl.pallas_call(
        _shared_kv_kernel,
        grid=(batch, seq // TOKEN_TILE),
        in_specs=[
            pl.BlockSpec((None, D_MODEL, TOKEN_TILE), lambda b, s: (b, 0, s)),
            full((D_MODEL, LANES)),
            full((2 * MAIN_WIDTH, D_MODEL)),
            pl.BlockSpec((HALF_HEAD, TOKEN_TILE), lambda b, s: (0, s)),
            pl.BlockSpec((HALF_HEAD, TOKEN_TILE), lambda b, s: (0, s)),
        ],
        out_specs=[
            pl.BlockSpec((None, MOBA_HEADS, per_tile, MOBA_BLOCK, HEAD_DIM), lambda b, s: (b, 0, s, 0, 0)),
            pl.BlockSpec((None, MOBA_HEADS, per_tile, HEAD_DIM, MOBA_BLOCK), lambda b, s: (b, 0, s, 0, 0)),
            pl.BlockSpec((None, None, MOBA_HEADS, per_tile, HEAD_DIM), lambda b, s: (b, s, 0, 0, 0)),
        ],
        out_shape=[
            jax.ShapeDtypeStruct((batch, MOBA_HEADS, nb, MOBA_BLOCK, HEAD_DIM), BF16),
            jax.ShapeDtypeStruct((batch, MOBA_HEADS, nb, HEAD_DIM, MOBA_BLOCK), BF16),
            jax.ShapeDtypeStruct((batch, seq // TOKEN_TILE, MOBA_HEADS, per_tile, HEAD_DIM), F32),
        ],
        compiler_params=pltpu.CompilerParams(
            dimension_semantics=("arbitrary", "arbitrary"), vmem_limit_bytes=VMEM_LIMIT),
        name="shared_kv",
    )(xt, g_lanes, wkv_t, cos_t, sin_t)
    k_blocks, vt_blocks, kmean = outs
    kmean = jnp.swapaxes(kmean, 1, 2).reshape(batch, MOBA_HEADS, nb, HEAD_DIM)
    return k_blocks, vt_blocks, kmean


def _moba_front_kernel(x_ref, g_ref, win_ref, cos_ref, sin_ref, memk_ref, memvt_ref,
                       q_ref, memout_ref):
    h = _rmsnorm_t(x_ref[...], g_ref[...]).astype(BF16)
    proj = _dot(win_ref[...], h)
    q_ref[...] = _rope_t(proj[:MAIN_WIDTH], cos_ref[...], sin_ref[...])
    memout_ref[...] = _memory_attention_t(proj[MAIN_WIDTH:], memk_ref, memvt_ref).astype(BF16)


def _moba_front(xt, g_lanes, win_t, cos_t, sin_t, mem_k, mem_vt):
    batch, _, seq = xt.shape
    full = lambda shape: pl.BlockSpec(shape, lambda b, s: (0,) * len(shape))
    return pl.pallas_call(
        _moba_front_kernel,
        grid=(batch, seq // TOKEN_TILE),
        in_specs=[
            pl.BlockSpec((None, D_MODEL, TOKEN_TILE), lambda b, s: (b, 0, s)),
            full((D_MODEL, LANES)),
            full((MIX_WIDTH, D_MODEL)),
            pl.BlockSpec((HALF_HEAD, TOKEN_TILE), lambda b, s: (0, s)),
            pl.BlockSpec((HALF_HEAD, TOKEN_TILE), lambda b, s: (0, s)),
            pl.BlockSpec((None, MEM_TOKENS, MEM_WIDTH), lambda b, s: (b, 0, 0)),
            pl.BlockSpec((None, MEM_WIDTH, MEM_TOKENS), lambda b, s: (b, 0, 0)),
        ],
        out_specs=[
            pl.BlockSpec((None, MAIN_WIDTH, TOKEN_TILE), lambda b, s: (b, 0, s)),
            pl.BlockSpec((None, MEM_WIDTH, TOKEN_TILE), lambda b, s: (b, 0, s)),
        ],
        out_shape=[
            jax.ShapeDtypeStruct((batch, MAIN_WIDTH, seq), F32),
            jax.ShapeDtypeStruct((batch, MEM_WIDTH, seq), BF16),
        ],
        compiler_params=pltpu.CompilerParams(
            dimension_semantics=("arbitrary", "arbitrary"), vmem_limit_bytes=VMEM_LIMIT),
        name="moba_front",
    )(xt, g_lanes, win_t, cos_t, sin_t, mem_k, mem_vt)


def _moba_kernel(q_ref, k_ref, vt_ref, kmean_ref, o_ref, sel_ref):
    n = pl.program_id(2)
    nb = kmean_ref.shape[0]
    q = q_ref[...]

    gate = _dot_hi(kmean_ref[...], q)
    blk = lax.broadcasted_iota(jnp.int32, gate.shape, 0)
    past = blk < n
    gate = jnp.where(past, gate, NEG)
    rank = jnp.zeros(gate.shape, F32)
    for m in range(nb):
        gm = gate[m:m + 1, :]
        ahead = jnp.where(gm > gate, 1.0, jnp.where((gm == gate) & (blk > m), 1.0, 0.0))
        rank = rank + ahead
    sel = jnp.where((rank < MOBA_TOPK) & past, 1.0, 0.0)
    for m in range(nb):
        sel_ref[m] = jnp.broadcast_to(sel[m:m + 1, :], sel_ref.shape[1:])

    qb = q.astype(BF16)
    kpos = lax.broadcasted_iota(jnp.int32, (MOBA_BLOCK, MOBA_BLOCK), 0)
    qpos = lax.broadcasted_iota(jnp.int32, (MOBA_BLOCK, MOBA_BLOCK), 1)
    s_own = jnp.where(kpos <= qpos, _dot(k_ref[n], qb) * ATTN_SCALE, NEG)
    m0 = jnp.max(s_own, axis=0, keepdims=True)
    p0 = jnp.exp(s_own - m0)
    l0 = jnp.sum(p0, axis=0, keepdims=True)
    acc0 = _dot(vt_ref[n], p0.astype(BF16))

    def body(j, carry):
        m_run, l_run, acc = carry
        picked = sel_ref[j][0:1, :] > 0.0
        s_j = jnp.where(picked, _dot(k_ref[j], qb) * ATTN_SCALE, NEG)
        m_new = jnp.maximum(m_run, jnp.max(s_j, axis=0, keepdims=True))
        alpha = jnp.exp(m_run - m_new)
        p = jnp.exp(s_j - m_new)
        l_new = alpha * l_run + jnp.sum(p, axis=0, keepdims=True)
        acc = alpha * acc + _dot(vt_ref[j], p.astype(BF16))
        return m_new, l_new, acc

    _, l_fin, acc = lax.fori_loop(0, n, body, (m0, l0, acc0))
    o_ref[...] = (acc * (1.0 / l_fin)).astype(BF16)


def _moba_attention(q_t, k_blocks, vt_blocks, kmean):
    batch, _, seq = q_t.shape
    nb = seq // MOBA_BLOCK
    return pl.pallas_call(
        _moba_kernel,
        grid=(batch, MOBA_HEADS, nb),
        in_specs=[
            pl.BlockSpec((None, HEAD_DIM, MOBA_BLOCK), lambda b, h, n: (b, h, n)),
            pl.BlockSpec((None, None, nb, MOBA_BLOCK, HEAD_DIM), lambda b, h, n: (b, h, 0, 0, 0)),
            pl.BlockSpec((None, None, nb, HEAD_DIM, MOBA_BLOCK), lambda b, h, n: (b, h, 0, 0, 0)),
            pl.BlockSpec((None, None, nb, HEAD_DIM), lambda b, h, n: (b, h, 0, 0)),
        ],
        out_specs=pl.BlockSpec((None, HEAD_DIM, MOBA_BLOCK), lambda b, h, n: (b, h, n)),
        out_shape=jax.ShapeDtypeStruct((batch, MAIN_WIDTH, seq), BF16),
        scratch_shapes=[pltpu.VMEM((nb, 8, MOBA_BLOCK), F32)],
        compiler_params=pltpu.CompilerParams(
            dimension_semantics=("arbitrary", "arbitrary", "arbitrary"),
            vmem_limit_bytes=VMEM_LIMIT),
        name="moba_attention",
    )(q_t, k_blocks, vt_blocks, kmean)


def _out_proj_kernel(x_ref, attn_ref, memout_ref, wout_ref, o_ref):
    cat = jnp.concatenate([attn_ref[...], memout_ref[...]], axis=0)
    o_ref[...] = x_ref[...] + _dot(wout_ref[...], cat)


def _out_proj(xt, attn_t, memout_t, wout_t):
    batch, _, seq = xt.shape
    tile = lambda rows: pl.BlockSpec((None, rows, TOKEN_TILE), lambda b, s: (b, 0, s))
    return pl.pallas_call(
        _out_proj_kernel,
        grid=(batch, seq // TOKEN_TILE),
        in_specs=[tile(D_MODEL), tile(MAIN_WIDTH), tile(MEM_WIDTH),
                  pl.BlockSpec((D_MODEL, MIX_WIDTH), lambda b, s: (0, 0))],
        out_specs=tile(D_MODEL),
        out_shape=jax.ShapeDtypeStruct(xt.shape, F32),
        compiler_params=pltpu.CompilerParams(
            dimension_semantics=("arbitrary", "arbitrary"), vmem_limit_bytes=VMEM_LIMIT),
        name="out_proj",
    )(xt, attn_t, memout_t, wout_t)


def _lanes(v):
    return jnp.broadcast_to(v.astype(F32)[:, None], (v.shape[0], LANES))


def _rope_tables_t(seq):
    inv = 1.0 / (ROPE_THETA ** (jnp.arange(0, HEAD_DIM, 2, dtype=F32) / HEAD_DIM))
    ang = jnp.arange(seq, dtype=F32)[:, None] * inv[None, :]
    return jnp.cos(ang).T, jnp.sin(ang).T


def kernel(x, mem, norm_mix_g, w_in, w_out, w_mem_kv, pool_w, pool_scale, kv_norm_g,
           w_kv_shared, norm_ffn_g, peer_wq, peer_subkeys, peer_u, peer_v, final_norm_g):
    seq = x.shape[1]
    cos_t, sin_t = _rope_tables_t(seq)

    win_t = jnp.swapaxes(w_in, 1, 2).astype(BF16)
    wout_t = jnp.swapaxes(w_out, 1, 2).astype(BF16)
    wq_t = jnp.swapaxes(peer_wq, 1, 2).astype(BF16)
    u_bf = peer_u.astype(BF16)
    vt_bf = jnp.swapaxes(peer_v, 1, 2).astype(BF16)
    wkv_t = w_kv_shared.T.astype(BF16)
    poolw_t = jnp.swapaxes(pool_w[0], 1, 2).astype(BF16)
    wk_mem = w_mem_kv[:, :, :MEM_WIDTH].astype(BF16)
    wvt_mem = jnp.swapaxes(w_mem_kv[:, :, MEM_WIDTH:], 1, 2).astype(BF16)

    xt = jnp.swapaxes(x, 1, 2)
    mem_bf = mem.astype(BF16)
    mem_k, mem_vt = _mem_kv(mem_bf, jnp.swapaxes(mem_bf, 1, 2), wk_mem, wvt_mem)

    xt = _pool_layer(xt, _lanes(norm_mix_g[0]), win_t[0], poolw_t, _lanes(pool_scale[0]),
                     mem_k[0], mem_vt[0], wout_t[0])
    xt = _peer_layer(xt, _lanes(norm_ffn_g[0]), wq_t[0], peer_subkeys[0], u_bf[0], vt_bf[0],
                     _lanes(final_norm_g), final_norm=False)

    k_blocks, vt_blocks, kmean = _shared_kv(xt, _lanes(kv_norm_g), wkv_t, cos_t, sin_t)

    q_t, memout_t = _moba_front(xt, _lanes(norm_mix_g[1]), win_t[1], cos_t, sin_t,
                                mem_k[1], mem_vt[1])
    attn_t = _moba_attention(q_t, k_blocks, vt_blocks, kmean)
    xt = _out_proj(xt, attn_t, memout_t, wout_t[1])
    yt = _peer_layer(xt, _lanes(norm_ffn_g[1]), wq_t[1], peer_subkeys[1], u_bf[1], vt_bf[1],
                     _lanes(final_norm_g), final_norm=True)
    return jnp.swapaxes(yt, 1, 2)
```

```python
import functools
import math

import jax
import jax.numpy as jnp
from jax import lax
from jax.experimental import pallas as pl
from jax.experimental.pallas import tpu as pltpu

F32 = jnp.float32
BF16 = jnp.bfloat16

D_MODEL = 1024
HEAD_DIM = 128
HALF_HEAD = HEAD_DIM // 2
MAIN_WIDTH = 1024
POOL_WINDOWS = (2, 4, 8, 16)
POOL_GROUP = MAIN_WIDTH // len(POOL_WINDOWS)
MOBA_HEADS = MAIN_WIDTH // HEAD_DIM
MOBA_BLOCK = 256
MOBA_TOPK = 3
MEM_TOKENS = 256
MEM_HEADS = 4
MEM_WIDTH = MEM_HEADS * HEAD_DIM
MIX_WIDTH = MAIN_WIDTH + MEM_WIDTH
ROPE_THETA = 10000.0
PEER_HEADS = 8
PEER_NKEYS = 128
PEER_EXPERTS = PEER_NKEYS * PEER_NKEYS
PEER_TOPK = 16
PEER_QDIM = 256
RMS_EPS = 1e-6
NEG = -1e30
ATTN_SCALE = HEAD_DIM ** -0.5

LANES = 128
TOKEN_TILE = 512
LANE_CHUNKS = TOKEN_TILE // LANES
EXPERT_TILE = 1024
EXPERT_ROWS = EXPERT_TILE // PEER_NKEYS
HALF_TILE = EXPERT_TILE // 2
HALF_ROWS = EXPERT_ROWS // 2
BF16_ROWS = 16
VMEM_LIMIT = 56 * 1024 * 1024
STAIR_COUNTS = tuple(PEER_TOPK // (a + 1) for a in range(PEER_TOPK))
STAIR_ROWS = sum(STAIR_COUNTS)
STAIR_PAD = -(-STAIR_ROWS // 8) * 8
RANK_NONE = 64.0


def _dot(a, b):
    return jnp.dot(a, b, preferred_element_type=F32)


def _dot_hi(a, b):
    return jnp.dot(a, b, preferred_element_type=F32, precision=lax.Precision.HIGHEST)


def _rmsnorm_t(x, g_lanes):
    ms = jnp.sum(x * x, axis=0, keepdims=True) * (1.0 / x.shape[0])
    y = x * lax.rsqrt(ms + RMS_EPS)
    chunks = [y[:, c * LANES:(c + 1) * LANES] * g_lanes for c in range(x.shape[1] // LANES)]
    return jnp.concatenate(chunks, axis=1)


def _scale_rows(x, s_lanes):
    chunks = [x[:, c * LANES:(c + 1) * LANES] * s_lanes for c in range(x.shape[1] // LANES)]
    return jnp.concatenate(chunks, axis=1)


def _rope_t(x, cos, sin):
    outs = []
    for h in range(x.shape[0] // HEAD_DIM):
        x1 = x[h * HEAD_DIM:h * HEAD_DIM + HALF_HEAD]
        x2 = x[h * HEAD_DIM + HALF_HEAD:(h + 1) * HEAD_DIM]
        outs.append(x1 * cos - x2 * sin)
        outs.append(x2 * cos + x1 * sin)
    return jnp.concatenate(outs, axis=0)


def _memory_attention_t(qm, mem_k_ref, mem_vt_ref):
    outs = []
    for h in range(MEM_HEADS):
        rows = slice(h * HEAD_DIM, (h + 1) * HEAD_DIM)
        q = qm[rows].astype(BF16)
        logits = _dot(mem_k_ref[:, rows], q) * ATTN_SCALE
        m = jnp.max(logits, axis=0, keepdims=True)
        e = jnp.exp(logits - m)
        z = jnp.sum(e, axis=0, keepdims=True)
        o = _dot(mem_vt_ref[rows, :], e.astype(BF16))
        outs.append(o * (1.0 / z))
    return jnp.concatenate(outs, axis=0)


def _mem_kv_kernel(mem_ref, memt_ref, wk_ref, wvt_ref, k_ref, vt_ref):
    k_ref[...] = _dot(mem_ref[...], wk_ref[...]).astype(BF16)
    vt_ref[...] = _dot(wvt_ref[...], memt_ref[...]).astype(BF16)


def _mem_kv(mem_bf, memt_bf, wk, wvt):
    n_layers, batch = wk.shape[0], mem_bf.shape[0]
    return pl.pallas_call(
        _mem_kv_kernel,
        grid=(n_layers, batch),
        in_specs=[
            pl.BlockSpec((None, MEM_TOKENS, D_MODEL), lambda l, b: (b, 0, 0)),
            pl.BlockSpec((None, D_MODEL, MEM_TOKENS), lambda l, b: (b, 0, 0)),
            pl.BlockSpec((None, D_MODEL, MEM_WIDTH), lambda l, b: (l, 0, 0)),
            pl.BlockSpec((None, MEM_WIDTH, D_MODEL), lambda l, b: (l, 0, 0)),
        ],
        out_specs=[
            pl.BlockSpec((None, None, MEM_TOKENS, MEM_WIDTH), lambda l, b: (l, b, 0, 0)),
            pl.BlockSpec((None, None, MEM_WIDTH, MEM_TOKENS), lambda l, b: (l, b, 0, 0)),
        ],
        out_shape=[
            jax.ShapeDtypeStruct((n_layers, batch, MEM_TOKENS, MEM_WIDTH), BF16),
            jax.ShapeDtypeStruct((n_layers, batch, MEM_WIDTH, MEM_TOKENS), BF16),
        ],
        name="mem_kv",
    )(mem_bf, memt_bf, wk, wvt)


def _pool_layer_kernel(x_ref, g_ref, win_ref, poolw_ref, pscale_ref, memk_ref, memvt_ref,
                       wout_ref, o_ref, halo_ref):
    s = pl.program_id(1)

    @pl.when(s == 0)
    def _():
        halo_ref[...] = jnp.zeros_like(halo_ref)

    x = x_ref[...]
    h = _rmsnorm_t(x, g_ref[...]).astype(BF16)
    proj = _dot(win_ref[...], h)
    main = proj[:MAIN_WIDTH]
    qm = proj[MAIN_WIDTH:]

    ext = jnp.concatenate([halo_ref[...], main], axis=1)
    halo_ref[...] = main[:, TOKEN_TILE - LANES:]
    tpos = s * TOKEN_TILE + lax.broadcasted_iota(jnp.int32, (1, TOKEN_TILE), 1)

    mixed = []
    win = ext
    for g, w in enumerate(POOL_WINDOWS):
        win = win[(POOL_GROUP if g else 0):]
        win = win + pltpu.roll(win, w // 2, axis=1)
        inv_cnt = 1.0 / jnp.minimum(tpos + 1, w).astype(F32)
        grp = slice(g * POOL_GROUP, (g + 1) * POOL_GROUP)
        d = win[:POOL_GROUP, LANES:] * inv_cnt - main[grp]
        mixed.append(_dot(poolw_ref[g], d.astype(BF16)))
    mixed = _scale_rows(jnp.concatenate(mixed, axis=0), pscale_ref[...])

    mem_out = _memory_attention_t(qm, memk_ref, memvt_ref)
    cat = jnp.concatenate([mixed, mem_out], axis=0).astype(BF16)
    o_ref[...] = x + _dot(wout_ref[...], cat)


def _pool_layer(xt, g_lanes, win_t, poolw_t, pscale_lanes, mem_k, mem_vt, wout_t):
    batch, _, seq = xt.shape
    tile = pl.BlockSpec((None, D_MODEL, TOKEN_TILE), lambda b, s: (b, 0, s))
    full = lambda shape: pl.BlockSpec(shape, lambda b, s: (0,) * len(shape))
    return pl.pallas_call(
        _pool_layer_kernel,
        grid=(batch, seq // TOKEN_TILE),
        in_specs=[
            tile,
            full((D_MODEL, LANES)),
            full((MIX_WIDTH, D_MODEL)),
            full((len(POOL_WINDOWS), POOL_GROUP, POOL_GROUP)),
            full((MAIN_WIDTH, LANES)),
            pl.BlockSpec((None, MEM_TOKENS, MEM_WIDTH), lambda b, s: (b, 0, 0)),
            pl.BlockSpec((None, MEM_WIDTH, MEM_TOKENS), lambda b, s: (b, 0, 0)),
            full((D_MODEL, MIX_WIDTH)),
        ],
        out_specs=tile,
        out_shape=jax.ShapeDtypeStruct(xt.shape, F32),
        scratch_shapes=[pltpu.VMEM((MAIN_WIDTH, LANES), F32)],
        compiler_params=pltpu.CompilerParams(
            dimension_semantics=("arbitrary", "arbitrary"), vmem_limit_bytes=VMEM_LIMIT),
        name="pool_layer",
    )(xt, g_lanes, win_t, poolw_t, pscale_lanes, mem_k, mem_vt, wout_t)


def _top16(s, track_rank):
    work = s
    rank = jnp.full(s.shape, RANK_NONE, F32)
    rows = []
    for k in range(PEER_TOPK):
        m = jnp.max(work, axis=0, keepdims=True)
        rows.append(m)
        hit = work == m
        if track_rank:
            rank = jnp.where(hit, float(k), rank)
        if k + 1 < PEER_TOPK:
            work = jnp.where(hit, -jnp.inf, work)
    return rows, rank


def _peer_select_chunk(s1, s2, sv2_ref, e2_ref, sums_ref, wts_ref):
    sv1, _ = _top16(s1, False)
    sv2, rank2 = _top16(s2, True)
    for k in range(PEER_TOPK):
        sv2_ref[k:k + 1, :] = sv2[k]
        e2_ref[k:k + 1, :] = jnp.exp(sv2[k] - sv2[0])
    off = 0
    for a, nb in enumerate(STAIR_COUNTS):
        sums_ref[off:off + nb, :] = sv2_ref[0:nb, :] + sv1[a]
        wts_ref[off:off + nb, :] = e2_ref[0:nb, :] * jnp.exp(sv1[a] - sv1[0])
        off += nb
    if STAIR_PAD > STAIR_ROWS:
        pad = (STAIR_PAD - STAIR_ROWS, LANES)
        sums_ref[STAIR_ROWS:STAIR_PAD, :] = jnp.full(pad, -jnp.inf, F32)
        wts_ref[STAIR_ROWS:STAIR_PAD, :] = jnp.zeros(pad, F32)

    sums = sums_ref[...]
    work = sums
    seen = jnp.zeros((1, LANES), F32)
    tau = jnp.full((1, LANES), -jnp.inf, F32)
    for k in range(PEER_TOPK):
        m = jnp.max(work, axis=0, keepdims=True)
        hit = work == m
        tau = jnp.where(seen < PEER_TOPK, m, tau)
        seen = seen + jnp.sum(jnp.where(hit, 1.0, 0.0), axis=0, keepdims=True)
        work = jnp.where(hit, -jnp.inf, work)
    z = jnp.sum(jnp.where(sums >= tau, wts_ref[...], 0.0), axis=0, keepdims=True)

    cnt = jnp.zeros(s1.shape, F32)
    for b in range(PEER_TOPK):
        cnt = cnt + jnp.where(s1 + sv2[b] >= tau, 1.0, 0.0)
    p1 = jnp.exp(s1 - sv1[0])
    r2 = jnp.exp(s2 - sv2[0]) * (1.0 / z)
    return p1, cnt, rank2, r2


def _gelu(a):
    return 0.5 * a * (1.0 + lax.erf(a * (1.0 / math.sqrt(2.0))))


def _peer_kernel(x_ref, g_ref, wq_ref, subk_ref, u_ref, vt_ref, gfin_ref, o_ref,
                 hn_ref, acc_ref, s_ref, p1_ref, cnt_ref, rank2_ref, r2_ref,
                 sv2_ref, e2_ref, sums_ref, wts_ref, h0_ref, h1_ref, *, final_norm):
    e = pl.program_id(2)

    @pl.when(e == 0)
    def _():
        hn_ref[...] = _rmsnorm_t(x_ref[...], g_ref[...]).astype(BF16)
        acc_ref[...] = jnp.zeros_like(acc_ref)

        def head_body(h, carry):
            row0 = pl.multiple_of(h * PEER_QDIM, PEER_QDIM)
            q = _dot(wq_ref[pl.ds(row0, PEER_QDIM), :], hn_ref[...])
            s_ref[0] = _dot_hi(subk_ref[h, 0], q[:PEER_NKEYS])
            s_ref[1] = _dot_hi(subk_ref[h, 1], q[PEER_NKEYS:])
            for c in range(LANE_CHUNKS):
                lanes = slice(c * LANES, (c + 1) * LANES)
                p1, cnt, rank2, r2 = _peer_select_chunk(
                    s_ref[0, :, lanes], s_ref[1, :, lanes], sv2_ref, e2_ref, sums_ref, wts_ref)
                p1_ref[h, :, lanes] = p1
                cnt_ref[h, :, lanes] = cnt
                rank2_ref[h, :, lanes] = rank2.astype(BF16)
                r2_ref[h, :, lanes] = r2.astype(BF16)
            return carry

        lax.fori_loop(0, PEER_HEADS, head_body, 0)

    def gate_tile(dst_ref, row_base, i0):
        for c in range(LANE_CHUNKS):
            lanes = slice(c * LANES, (c + 1) * LANES)
            cnt_blk = [cnt_ref[h, i0, lanes] for h in range(PEER_HEADS)]
            p1_blk = [p1_ref[h, i0, lanes] for h in range(PEER_HEADS)]
            for r in range(HALF_ROWS):
                row = row_base + r
                cnt_b = [jnp.broadcast_to(cnt_blk[h][row:row + 1], (BF16_ROWS, LANES)).astype(BF16)
                         for h in range(PEER_HEADS)]
                p1_b = [jnp.broadcast_to(p1_blk[h][row:row + 1], (BF16_ROWS, LANES)).astype(BF16)
                        for h in range(PEER_HEADS)]
                for jg in range(PEER_NKEYS // BF16_ROWS):
                    keys = slice(jg * BF16_ROWS, (jg + 1) * BF16_ROWS)
                    gate = jnp.zeros((BF16_ROWS, LANES), BF16)
                    for h in range(PEER_HEADS):
                        r2 = r2_ref[h, keys, lanes]
                        picked = jnp.where(rank2_ref[h, keys, lanes] < cnt_b[h], r2, jnp.zeros_like(r2))
                        gate = gate + picked * p1_b[h]
                    dst_ref[r * PEER_NKEYS + jg * BF16_ROWS:r * PEER_NKEYS + (jg + 1) * BF16_ROWS,
                            lanes] = gate

    def gelu_tile(dst_ref, a):
        for r in range(HALF_ROWS):
            rows = slice(r * PEER_NKEYS, (r + 1) * PEER_NKEYS)
            for c in range(LANE_CHUNKS):
                lanes = slice(c * LANES, (c + 1) * LANES)
                dst_ref[rows, lanes] = _gelu(a[rows, lanes]).astype(BF16) * dst_ref[rows, lanes]

    i0 = pl.ds(pl.multiple_of(e * EXPERT_ROWS, EXPERT_ROWS), EXPERT_ROWS)
    a0 = _dot(u_ref[:HALF_TILE, :], hn_ref[...])
    gate_tile(h0_ref, 0, i0)
    gelu_tile(h0_ref, a0)
    a1 = _dot(u_ref[HALF_TILE:, :], hn_ref[...])
    gate_tile(h1_ref, HALF_ROWS, i0)
    acc_ref[...] += _dot(vt_ref[:, :HALF_TILE], h0_ref[...])
    gelu_tile(h1_ref, a1)
    acc_ref[...] += _dot(vt_ref[:, HALF_TILE:], h1_ref[...])

    @pl.when(e == pl.num_programs(2) - 1)
    def _():
        y = x_ref[...] + acc_ref[...]
        if final_norm:
            y = _rmsnorm_t(y, gfin_ref[...])
        o_ref[...] = y


def _peer_layer(xt, g_lanes, wq_t, subkeys, u_bf, vt_bf, gfin_lanes, final_norm):
    batch, _, seq = xt.shape
    n_steps = PEER_EXPERTS // EXPERT_TILE
    tile = pl.BlockSpec((None, D_MODEL, TOKEN_TILE), lambda b, s, e: (b, 0, s))
    full = lambda shape: pl.BlockSpec(shape, lambda b, s, e: (0,) * len(shape))
    sel = lambda dtype: pltpu.VMEM((PEER_HEADS, PEER_NKEYS, TOKEN_TILE), dtype)
    return pl.pallas_call(
        functools.partial(_peer_kernel, final_norm=final_norm),
        grid=(batch, seq // TOKEN_TILE, n_steps),
        in_specs=[
            tile,
            full((D_MODEL, LANES)),
            full((PEER_HEADS * PEER_QDIM, D_MODEL)),
            full((PEER_HEADS, 2, PEER_NKEYS, PEER_NKEYS)),
            pl.BlockSpec((EXPERT_TILE, D_MODEL), lambda b, s, e: (e, 0)),
            pl.BlockSpec((D_MODEL, EXPERT_TILE), lambda b, s, e: (0, e)),
            full((D_MODEL, LANES)),
        ],
        out_specs=tile,
        out_shape=jax.ShapeDtypeStruct(xt.shape, F32),
        scratch_shapes=[
            pltpu.VMEM((D_MODEL, TOKEN_TILE), BF16),
            pltpu.VMEM((D_MODEL, TOKEN_TILE), F32),
            pltpu.VMEM((2, PEER_NKEYS, TOKEN_TILE), F32),
            sel(F32), sel(F32), sel(BF16), sel(BF16),
            pltpu.VMEM((PEER_TOPK, LANES), F32),
            pltpu.VMEM((PEER_TOPK, LANES), F32),
            pltpu.VMEM((STAIR_PAD, LANES), F32),
            pltpu.VMEM((STAIR_PAD, LANES), F32),
            pltpu.VMEM((HALF_TILE, TOKEN_TILE), BF16),
            pltpu.VMEM((HALF_TILE, TOKEN_TILE), BF16),
        ],
        compiler_params=pltpu.CompilerParams(
            dimension_semantics=("arbitrary", "arbitrary", "arbitrary"),
            vmem_limit_bytes=VMEM_LIMIT),
        name="peer_final" if final_norm else "peer",
    )(xt, g_lanes, wq_t, subkeys, u_bf, vt_bf, gfin_lanes)


def _shared_kv_kernel(x_ref, g_ref, wkv_ref, cos_ref, sin_ref, k_ref, vt_ref, kmean_ref):
    h = _rmsnorm_t(x_ref[...], g_ref[...]).astype(BF16)
    kv = _dot(wkv_ref[...], h)
    k_t = _rope_t(kv[:MAIN_WIDTH], cos_ref[...], sin_ref[...])
    v_t = kv[MAIN_WIDTH:]
    k = k_t.T
    for jb in range(TOKEN_TILE // MOBA_BLOCK):
        toks = slice(jb * MOBA_BLOCK, (jb + 1) * MOBA_BLOCK)
        for hh in range(MOBA_HEADS):
            cols = slice(hh * HEAD_DIM, (hh + 1) * HEAD_DIM)
            kb = k[toks, cols]
            k_ref[hh, jb] = kb.astype(BF16)
            vt_ref[hh, jb] = v_t[cols, toks].astype(BF16)
            kmean_ref[hh, jb:jb + 1, :] = jnp.sum(kb, axis=0, keepdims=True) * (1.0 / MOBA_BLOCK)


def _shared_kv(xt, g_lanes, wkv_t, cos_t, sin_t):
    batch, _, seq = xt.shape
    nb = seq // MOBA_BLOCK
    per_tile = TOKEN_TILE // MOBA_BLOCK
    full = lambda shape: pl.BlockSpec(shape, lambda b, s: (0,) * len(shape))
    outs = pl.pallas_call(
        _shared_kv_kernel,
        grid=(batch, seq // TOKEN_TILE),
        in_specs=[
            pl.BlockSpec((None, D_MODEL, TOKEN_TILE), lambda b, s: (b, 0, s)),
            full((D_MODEL, LANES)),
            full((2 * MAIN_WIDTH, D_MODEL)),
            pl.BlockSpec((HALF_HEAD, TOKEN_TILE), lambda b, s: (0, s)),
            pl.BlockSpec((HALF_HEAD, TOKEN_TILE), lambda b, s: (0, s)),
        ],
        out_specs=[
            pl.BlockSpec((None, MOBA_HEADS, per_tile, MOBA_BLOCK, HEAD_DIM), lambda b, s: (b, 0, s, 0, 0)),
            pl.BlockSpec((None, MOBA_HEADS, per_tile, HEAD_DIM, MOBA_BLOCK), lambda b, s: (b, 0, s, 0, 0)),
            pl.BlockSpec((None, None, MOBA_HEADS, per_tile, HEAD_DIM), lambda b, s: (b, s, 0, 0, 0)),
        ],
        out_shape=[
            jax.ShapeDtypeStruct((batch, MOBA_HEADS, nb, MOBA_BLOCK, HEAD_DIM), BF16),
            jax.ShapeDtypeStruct((batch, MOBA_HEADS, nb, HEAD_DIM, MOBA_BLOCK), BF16),
            jax.ShapeDtypeStruct((batch, seq // TOKEN_TILE, MOBA_HEADS, per_tile, HEAD_DIM), F32),
        ],
        compiler_params=pltpu.CompilerParams(
            dimension_semantics=("arbitrary", "arbitrary"), vmem_limit_bytes=VMEM_LIMIT),
        name="shared_kv",
    )(xt, g_lanes, wkv_t, cos_t, sin_t)
    k_blocks, vt_blocks, kmean = outs
    kmean = jnp.swapaxes(kmean, 1, 2).reshape(batch, MOBA_HEADS, nb, HEAD_DIM)
    return k_blocks, vt_blocks, kmean


def _moba_front_kernel(x_ref, g_ref, win_ref, cos_ref, sin_ref, memk_ref, memvt_ref,
                       q_ref, memout_ref):
    h = _rmsnorm_t(x_ref[...], g_ref[...]).astype(BF16)
    proj = _dot(win_ref[...], h)
    q_ref[...] = _rope_t(proj[:MAIN_WIDTH], cos_ref[...], sin_ref[...])
    memout_ref[...] = _memory_attention_t(proj[MAIN_WIDTH:], memk_ref, memvt_ref).astype(BF16)


def _moba_front(xt, g_lanes, win_t, cos_t, sin_t, mem_k, mem_vt):
    batch, _, seq = xt.shape
    full = lambda shape: pl.BlockSpec(shape, lambda b, s: (0,) * len(shape))
    return pl.pallas_call(
        _moba_front_kernel,
        grid=(batch, seq // TOKEN_TILE),
        in_specs=[
            pl.BlockSpec((None, D_MODEL, TOKEN_TILE), lambda b, s: (b, 0, s)),
            full((D_MODEL, LANES)),
            full((MIX_WIDTH, D_MODEL)),
            pl.BlockSpec((HALF_HEAD, TOKEN_TILE), lambda b, s: (0, s)),
            pl.BlockSpec((HALF_HEAD, TOKEN_TILE), lambda b, s: (0, s)),
            pl.BlockSpec((None, MEM_TOKENS, MEM_WIDTH), lambda b, s: (b, 0, 0)),
            pl.BlockSpec((None, MEM_WIDTH, MEM_TOKENS), lambda b, s: (b, 0, 0)),
        ],
        out_specs=[
            pl.BlockSpec((None, MAIN_WIDTH, TOKEN_TILE), lambda b, s: (b, 0, s)),
            pl.BlockSpec((None, MEM_WIDTH, TOKEN_TILE), lambda b, s: (b, 0, s)),
        ],
        out_shape=[
            jax.ShapeDtypeStruct((batch, MAIN_WIDTH, seq), F32),
            jax.ShapeDtypeStruct((batch, MEM_WIDTH, seq), BF16),
        ],
        compiler_params=pltpu.CompilerParams(
            dimension_semantics=("arbitrary", "arbitrary"), vmem_limit_bytes=VMEM_LIMIT),
        name="moba_front",
    )(xt, g_lanes, win_t, cos_t, sin_t, mem_k, mem_vt)


def _moba_kernel(q_ref, k_ref, vt_ref, kmean_ref, o_ref, sel_ref):
    n = pl.program_id(2)
    nb = kmean_ref.shape[0]
    q = q_ref[...]

    gate = _dot_hi(kmean_ref[...], q)
    blk = lax.broadcasted_iota(jnp.int32, gate.shape, 0)
    past = blk < n
    gate = jnp.where(past, gate, NEG)
    rank = jnp.zeros(gate.shape, F32)
    for m in range(nb):
        gm = gate[m:m + 1, :]
        ahead = jnp.where(gm > gate, 1.0, jnp.where((gm == gate) & (blk > m), 1.0, 0.0))
        rank = rank + ahead
    sel = jnp.where((rank < MOBA_TOPK) & past, 1.0, 0.0)
    for m in range(nb):
        sel_ref[m] = jnp.broadcast_to(sel[m:m + 1, :], sel_ref.shape[1:])

    qb = q.astype(BF16)
    kpos = lax.broadcasted_iota(jnp.int32, (MOBA_BLOCK, MOBA_BLOCK), 0)
    qpos = lax.broadcasted_iota(jnp.int32, (MOBA_BLOCK, MOBA_BLOCK), 1)
    s_own = jnp.where(kpos <= qpos, _dot(k_ref[n], qb) * ATTN_SCALE, NEG)
    m0 = jnp.max(s_own, axis=0, keepdims=True)
    p0 = jnp.exp(s_own - m0)
    l0 = jnp.sum(p0, axis=0, keepdims=True)
    acc0 = _dot(vt_ref[n], p0.astype(BF16))

    def body(j, carry):
        m_run, l_run, acc = carry
        picked = sel_ref[j][0:1, :] > 0.0
        s_j = jnp.where(picked, _dot(k_ref[j], qb) * ATTN_SCALE, NEG)
        m_new = jnp.maximum(m_run, jnp.max(s_j, axis=0, keepdims=True))
        alpha = jnp.exp(m_run - m_new)
        p = jnp.exp(s_j - m_new)
        l_new = alpha * l_run + jnp.sum(p, axis=0, keepdims=True)
        acc = alpha * acc + _dot(vt_ref[j], p.astype(BF16))
        return m_new, l_new, acc

    _, l_fin, acc = lax.fori_loop(0, n, body, (m0, l0, acc0))
    o_ref[...] = (acc * (1.0 / l_fin)).astype(BF16)


def _moba_attention(q_t, k_blocks, vt_blocks, kmean):
    batch, _, seq = q_t.shape
    nb = seq // MOBA_BLOCK
    return pl.pallas_call(
        _moba_kernel,
        grid=(batch, MOBA_HEADS, nb),
        in_specs=[
            pl.BlockSpec((None, HEAD_DIM, MOBA_BLOCK), lambda b, h, n: (b, h, n)),
            pl.BlockSpec((None, None, nb, MOBA_BLOCK, HEAD_DIM), lambda b, h, n: (b, h, 0, 0, 0)),
            pl.BlockSpec((None, None, nb, HEAD_DIM, MOBA_BLOCK), lambda b, h, n: (b, h, 0, 0, 0)),
            pl.BlockSpec((None, None, nb, HEAD_DIM), lambda b, h, n: (b, h, 0, 0)),
        ],
        out_specs=pl.BlockSpec((None, HEAD_DIM, MOBA_BLOCK), lambda b, h, n: (b, h, n)),
        out_shape=jax.ShapeDtypeStruct((batch, MAIN_WIDTH, seq), BF16),
        scratch_shapes=[pltpu.VMEM((nb, 8, MOBA_BLOCK), F32)],
        compiler_params=pltpu.CompilerParams(
            dimension_semantics=("arbitrary", "arbitrary", "arbitrary"),
            vmem_limit_bytes=VMEM_LIMIT),
        name="moba_attention",
    )(q_t, k_blocks, vt_blocks, kmean)


def _out_proj_kernel(x_ref, attn_ref, memout_ref, wout_ref, o_ref):
    cat = jnp.concatenate([attn_ref[...], memout_ref[...]], axis=0)
    o_ref[...] = x_ref[...] + _dot(wout_ref[...], cat)


def _out_proj(xt, attn_t, memout_t, wout_t):
    batch, _, seq = xt.shape
    tile = lambda rows: pl.BlockSpec((None, rows, TOKEN_TILE), lambda b, s: (b, 0, s))
    return pl.pallas_call(
        _out_proj_kernel,
        grid=(batch, seq // TOKEN_TILE),
        in_specs=[tile(D_MODEL), tile(MAIN_WIDTH), tile(MEM_WIDTH),
                  pl.BlockSpec((D_MODEL, MIX_WIDTH), lambda b, s: (0, 0))],
        out_specs=tile(D_MODEL),
        out_shape=jax.ShapeDtypeStruct(xt.shape, F32),
        compiler_params=pltpu.CompilerParams(
            dimension_semantics=("arbitrary", "arbitrary"), vmem_limit_bytes=VMEM_LIMIT),
        name="out_proj",
    )(xt, attn_t, memout_t, wout_t)


def _lanes(v):
    return jnp.broadcast_to(v.astype(F32)[:, None], (v.shape[0], LANES))


def _rope_tables_t(seq):
    inv = 1.0 / (ROPE_THETA ** (jnp.arange(0, HEAD_DIM, 2, dtype=F32) / HEAD_DIM))
    ang = jnp.arange(seq, dtype=F32)[:, None] * inv[None, :]
    return jnp.cos(ang).T, jnp.sin(ang).T


def kernel(x, mem, norm_mix_g, w_in, w_out, w_mem_kv, pool_w, pool_scale, kv_norm_g,
           w_kv_shared, norm_ffn_g, peer_wq, peer_subkeys, peer_u, peer_v, final_norm_g):
    seq = x.shape[1]
    cos_t, sin_t = _rope_tables_t(seq)

    win_t = jnp.swapaxes(w_in, 1, 2).astype(BF16)
    wout_t = jnp.swapaxes(w_out, 1, 2).astype(BF16)
    wq_t = jnp.swapaxes(peer_wq, 1, 2).astype(BF16)
    u_bf = peer_u.astype(BF16)
    vt_bf = jnp.swapaxes(peer_v, 1, 2).astype(BF16)
    wkv_t = w_kv_shared.T.astype(BF16)
    poolw_t = jnp.swapaxes(pool_w[0], 1, 2).astype(BF16)
    wk_mem = w_mem_kv[:, :, :MEM_WIDTH].astype(BF16)
    wvt_mem = jnp.swapaxes(w_mem_kv[:, :, MEM_WIDTH:], 1, 2).astype(BF16)

    xt = jnp.swapaxes(x, 1, 2)
    mem_bf = mem.astype(BF16)
    mem_k, mem_vt = _mem_kv(mem_bf, jnp.swapaxes(mem_bf, 1, 2), wk_mem, wvt_mem)

    xt = _pool_layer(xt, _lanes(norm_mix_g[0]), win_t[0], poolw_t, _lanes(pool_scale[0]),
                     mem_k[0], mem_vt[0], wout_t[0])
    xt = _peer_layer(xt, _lanes(norm_ffn_g[0]), wq_t[0], peer_subkeys[0], u_bf[0], vt_bf[0],
                     _lanes(final_norm_g), final_norm=False)

    k_blocks, vt_blocks, kmean = _shared_kv(xt, _lanes(kv_norm_g), wkv_t, cos_t, sin_t)

    q_t, memout_t = _moba_front(xt, _lanes(norm_mix_g[1]), win_t[1], cos_t, sin_t,
                                mem_k[1], mem_vt[1])
    attn_t = _moba_attention(q_t, k_blocks, vt_blocks, kmean)
    xt = _out_proj(xt, attn_t, memout_t, wout_t[1])
    yt = _peer_layer(xt, _lanes(norm_ffn_g[1]), wq_t[1], peer_subkeys[1], u_bf[1], vt_bf[1],
                     _lanes(final_norm_g), final_norm=True)
    return jnp.swapaxes(yt, 1, 2)
```

```python
import functools
import math

import jax
import jax.numpy as jnp
from jax import lax
from jax.experimental import pallas as pl
from jax.experimental.pallas import tpu as pltpu

F32 = jnp.float32
BF16 = jnp.bfloat16

D_MODEL = 1024
HEAD_DIM = 128
HALF_HEAD = HEAD_DIM // 2
MAIN_WIDTH = 1024
POOL_WINDOWS = (2, 4, 8, 16)
POOL_GROUP = MAIN_WIDTH // len(POOL_WINDOWS)
MOBA_HEADS = MAIN_WIDTH // HEAD_DIM
MOBA_BLOCK = 256
MOBA_TOPK = 3
MEM_TOKENS = 256
MEM_HEADS = 4
MEM_WIDTH = MEM_HEADS * HEAD_DIM
MIX_WIDTH = MAIN_WIDTH + MEM_WIDTH
ROPE_THETA = 10000.0
PEER_HEADS = 8
PEER_NKEYS = 128
PEER_EXPERTS = PEER_NKEYS * PEER_NKEYS
PEER_TOPK = 16
PEER_QDIM = 256
RMS_EPS = 1e-6
NEG = -1e30
ATTN_SCALE = HEAD_DIM ** -0.5

LANES = 128
TOKEN_TILE = 512
LANE_CHUNKS = TOKEN_TILE // LANES
EXPERT_TILE = 1024
EXPERT_ROWS = EXPERT_TILE // PEER_NKEYS
HALF_TILE = EXPERT_TILE // 2
HALF_ROWS = EXPERT_ROWS // 2
BF16_ROWS = 16
VMEM_LIMIT = 56 * 1024 * 1024
STAIR_COUNTS = tuple(PEER_TOPK // (a + 1) for a in range(PEER_TOPK))
STAIR_ROWS = sum(STAIR_COUNTS)
STAIR_PAD = -(-STAIR_ROWS // 8) * 8
RANK_NONE = 64.0


def _dot(a, b):
    return jnp.dot(a, b, preferred_element_type=F32)


def _dot_hi(a, b):
    return jnp.dot(a, b, preferred_element_type=F32, precision=lax.Precision.HIGHEST)


def _rmsnorm_t(x, g_lanes):
    ms = jnp.sum(x * x, axis=0, keepdims=True) * (1.0 / x.shape[0])
    y = x * lax.rsqrt(ms + RMS_EPS)
    chunks = [y[:, c * LANES:(c + 1) * LANES] * g_lanes for c in range(x.shape[1] // LANES)]
    return jnp.concatenate(chunks, axis=1)


def _scale_rows(x, s_lanes):
    chunks = [x[:, c * LANES:(c + 1) * LANES] * s_lanes for c in range(x.shape[1] // LANES)]
    return jnp.concatenate(chunks, axis=1)


def _rope_t(x, cos, sin):
    outs = []
    for h in range(x.shape[0] // HEAD_DIM):
        x1 = x[h * HEAD_DIM:h * HEAD_DIM + HALF_HEAD]
        x2 = x[h * HEAD_DIM + HALF_HEAD:(h + 1) * HEAD_DIM]
        outs.append(x1 * cos - x2 * sin)
        outs.append(x2 * cos + x1 * sin)
    return jnp.concatenate(outs, axis=0)


def _memory_attention_t(qm, mem_k_ref, mem_vt_ref):
    outs = []
    for h in range(MEM_HEADS):
        rows = slice(h * HEAD_DIM, (h + 1) * HEAD_DIM)
        q = qm[rows].astype(BF16)
        logits = _dot(mem_k_ref[:, rows], q) * ATTN_SCALE
        m = jnp.max(logits, axis=0, keepdims=True)
        e = jnp.exp(logits - m)
        z = jnp.sum(e, axis=0, keepdims=True)
        o = _dot(mem_vt_ref[rows, :], e.astype(BF16))
        outs.append(o * (1.0 / z))
    return jnp.concatenate(outs, axis=0)


def _mem_kv_kernel(mem_ref, memt_ref, wk_ref, wvt_ref, k_ref, vt_ref):
    k_ref[...] = _dot(mem_ref[...], wk_ref[...]).astype(BF16)
    vt_ref[...] = _dot(wvt_ref[...], memt_ref[...]).astype(BF16)


def _mem_kv(mem_bf, memt_bf, wk, wvt):
    n_layers, batch = wk.shape[0], mem_bf.shape[0]
    return pl.pallas_call(
        _mem_kv_kernel,
        grid=(n_layers, batch),
        in_specs=[
            pl.BlockSpec((None, MEM_TOKENS, D_MODEL), lambda l, b: (b, 0, 0)),
            pl.BlockSpec((None, D_MODEL, MEM_TOKENS), lambda l, b: (b, 0, 0)),
            pl.BlockSpec((None, D_MODEL, MEM_WIDTH), lambda l, b: (l, 0, 0)),
            pl.BlockSpec((None, MEM_WIDTH, D_MODEL), lambda l, b: (l, 0, 0)),
        ],
        out_specs=[
            pl.BlockSpec((None, None, MEM_TOKENS, MEM_WIDTH), lambda l, b: (l, b, 0, 0)),
            pl.BlockSpec((None, None, MEM_WIDTH, MEM_TOKENS), lambda l, b: (l, b, 0, 0)),
        ],
        out_shape=[
            jax.ShapeDtypeStruct((n_layers, batch, MEM_TOKENS, MEM_WIDTH), BF16),
            jax.ShapeDtypeStruct((n_layers, batch, MEM_WIDTH, MEM_TOKENS), BF16),
        ],
        name="mem_kv",
    )(mem_bf, memt_bf, wk, wvt)


def _pool_layer_kernel(x_ref, g_ref, win_ref, poolw_ref, pscale_ref, memk_ref, memvt_ref,
                       wout_ref, o_ref, halo_ref):
    s = pl.program_id(1)

    @pl.when(s == 0)
    def _():
        halo_ref[...] = jnp.zeros_like(halo_ref)

    x = x_ref[...]
    h = _rmsnorm_t(x, g_ref[...]).astype(BF16)
    proj = _dot(win_ref[...], h)
    main = proj[:MAIN_WIDTH]
    qm = proj[MAIN_WIDTH:]

    ext = jnp.concatenate([halo_ref[...], main], axis=1)
    halo_ref[...] = main[:, TOKEN_TILE - LANES:]
    tpos = s * TOKEN_TILE + lax.broadcasted_iota(jnp.int32, (1, TOKEN_TILE), 1)

    mixed = []
    win = ext
    for g, w in enumerate(POOL_WINDOWS):
        win = win[(POOL_GROUP if g else 0):]
        win = win + pltpu.roll(win, w // 2, axis=1)
        inv_cnt = 1.0 / jnp.minimum(tpos + 1, w).astype(F32)
        grp = slice(g * POOL_GROUP, (g + 1) * POOL_GROUP)
        d = win[:POOL_GROUP, LANES:] * inv_cnt - main[grp]
        mixed.append(_dot(poolw_ref[g], d.astype(BF16)))
    mixed = _scale_rows(jnp.concatenate(mixed, axis=0), pscale_ref[...])

    mem_out = _memory_attention_t(qm, memk_ref, memvt_ref)
    cat = jnp.concatenate([mixed, mem_out], axis=0).astype(BF16)
    o_ref[...] = x + _dot(wout_ref[...], cat)


def _pool_layer(xt, g_lanes, win_t, poolw_t, pscale_lanes, mem_k, mem_vt, wout_t):
    batch, _, seq = xt.shape
    tile = pl.BlockSpec((None, D_MODEL, TOKEN_TILE), lambda b, s: (b, 0, s))
    full = lambda shape: pl.BlockSpec(shape, lambda b, s: (0,) * len(shape))
    return pl.pallas_call(
        _pool_layer_kernel,
        grid=(batch, seq // TOKEN_TILE),
        in_specs=[
            tile,
            full((D_MODEL, LANES)),
            full((MIX_WIDTH, D_MODEL)),
            full((len(POOL_WINDOWS), POOL_GROUP, POOL_GROUP)),
            full((MAIN_WIDTH, LANES)),
            pl.BlockSpec((None, MEM_TOKENS, MEM_WIDTH), lambda b, s: (b, 0, 0)),
            pl.BlockSpec((None, MEM_WIDTH, MEM_TOKENS), lambda b, s: (b, 0, 0)),
            full((D_MODEL, MIX_WIDTH)),
        ],
        out_specs=tile,
        out_shape=jax.ShapeDtypeStruct(xt.shape, F32),
        scratch_shapes=[pltpu.VMEM((MAIN_WIDTH, LANES), F32)],
        compiler_params=pltpu.CompilerParams(
            dimension_semantics=("arbitrary", "arbitrary"), vmem_limit_bytes=VMEM_LIMIT),
        name="pool_layer",
    )(xt, g_lanes, win_t, poolw_t, pscale_lanes, mem_k, mem_vt, wout_t)


def _top16(s, track_rank):
    work = s
    rank = jnp.full(s.shape, RANK_NONE, F32)
    rows = []
    for k in range(PEER_TOPK):
        m = jnp.max(work, axis=0, keepdims=True)
        rows.append(m)
        hit = work == m
        if track_rank:
            rank = jnp.where(hit, float(k), rank)
        if k + 1 < PEER_TOPK:
            work = jnp.where(hit, -jnp.inf, work)
    return rows, rank


def _peer_select_chunk(s1, s2, sv2_ref, e2_ref, sums_ref, wts_ref):
    sv1, _ = _top16(s1, False)
    sv2, rank2 = _top16(s2, True)
    for k in range(PEER_TOPK):
        sv2_ref[k:k + 1, :] = sv2[k]
        e2_ref[k:k + 1, :] = jnp.exp(sv2[k] - sv2[0])
    off = 0
    for a, nb in enumerate(STAIR_COUNTS):
        sums_ref[off:off + nb, :] = sv2_ref[0:nb, :] + sv1[a]
        wts_ref[off:off + nb, :] = e2_ref[0:nb, :] * jnp.exp(sv1[a] - sv1[0])
        off += nb
    if STAIR_PAD > STAIR_ROWS:
        pad = (STAIR_PAD - STAIR_ROWS, LANES)
        sums_ref[STAIR_ROWS:STAIR_PAD, :] = jnp.full(pad, -jnp.inf, F32)
        wts_ref[STAIR_ROWS:STAIR_PAD, :] = jnp.zeros(pad, F32)

    sums = sums_ref[...]
    work = sums
    seen = jnp.zeros((1, LANES), F32)
    tau = jnp.full((1, LANES), -jnp.inf, F32)
    for k in range(PEER_TOPK):
        m = jnp.max(work, axis=0, keepdims=True)
        hit = work == m
        tau = jnp.where(seen < PEER_TOPK, m, tau)
        seen = seen + jnp.sum(jnp.where(hit, 1.0, 0.0), axis=0, keepdims=True)
        work = jnp.where(hit, -jnp.inf, work)
    z = jnp.sum(jnp.where(sums >= tau, wts_ref[...], 0.0), axis=0, keepdims=True)

    cnt = jnp.zeros(s1.shape, F32)
    for b in range(PEER_TOPK):
        cnt = cnt + jnp.where(s1 + sv2[b] >= tau, 1.0, 0.0)
    p1 = jnp.exp(s1 - sv1[0])
    r2 = jnp.exp(s2 - sv2[0]) * (1.0 / z)
    return p1, cnt, rank2, r2


def _gelu(a):
    return 0.5 * a * (1.0 + lax.erf(a * (1.0 / math.sqrt(2.0))))


def _peer_kernel(x_ref, g_ref, wq_ref, subk_ref, u_ref, vt_ref, gfin_ref, o_ref,
                 hn_ref, acc_ref, s_ref, p1_ref, cnt_ref, rank2_ref, r2_ref,
                 sv2_ref, e2_ref, sums_ref, wts_ref, h0_ref, h1_ref, *, final_norm):
    e = pl.program_id(2)

    @pl.when(e == 0)
    def _():
        hn_ref[...] = _rmsnorm_t(x_ref[...], g_ref[...]).astype(BF16)
        acc_ref[...] = jnp.zeros_like(acc_ref)

        def head_body(h, carry):
            row0 = pl.multiple_of(h * PEER_QDIM, PEER_QDIM)
            q = _dot(wq_ref[pl.ds(row0, PEER_QDIM), :], hn_ref[...])
            s_ref[0] = _dot_hi(subk_ref[h, 0], q[:PEER_NKEYS])
            s_ref[1] = _dot_hi(subk_ref[h, 1], q[PEER_NKEYS:])
            for c in range(LANE_CHUNKS):
                lanes = slice(c * LANES, (c + 1) * LANES)
                p1, cnt, rank2, r2 = _peer_select_chunk(
                    s_ref[0, :, lanes], s_ref[1, :, lanes], sv2_ref, e2_ref, sums_ref, wts_ref)
                p1_ref[h, :, lanes] = p1
                cnt_ref[h, :, lanes] = cnt
                rank2_ref[h, :, lanes] = rank2.astype(BF16)
                r2_ref[h, :, lanes] = r2.astype(BF16)
            return carry

        lax.fori_loop(0, PEER_HEADS, head_body, 0)

    def gate_tile(dst_ref, row_base, i0):
        for c in range(LANE_CHUNKS):
            lanes = slice(c * LANES, (c + 1) * LANES)
            cnt_blk = [cnt_ref[h, i0, lanes] for h in range(PEER_HEADS)]
            p1_blk = [p1_ref[h, i0, lanes] for h in range(PEER_HEADS)]
            for r in range(HALF_ROWS):
                row = row_base + r
                cnt_b = [jnp.broadcast_to(cnt_blk[h][row:row + 1], (BF16_ROWS, LANES)).astype(BF16)
                         for h in range(PEER_HEADS)]
                p1_b = [jnp.broadcast_to(p1_blk[h][row:row + 1], (BF16_ROWS, LANES)).astype(BF16)
                        for h in range(PEER_HEADS)]
                for jg in range(PEER_NKEYS // BF16_ROWS):
                    keys = slice(jg * BF16_ROWS, (jg + 1) * BF16_ROWS)
                    gate = jnp.zeros((BF16_ROWS, LANES), BF16)
                    for h in range(PEER_HEADS):
                        r2 = r2_ref[h, keys, lanes]
                        picked = jnp.where(rank2_ref[h, keys, lanes] < cnt_b[h], r2, jnp.zeros_like(r2))
                        gate = gate + picked * p1_b[h]
                    dst_ref[r * PEER_NKEYS + jg * BF16_ROWS:r * PEER_NKEYS + (jg + 1) * BF16_ROWS,
                            lanes] = gate

    def gelu_tile(dst_ref, a):
        for r in range(HALF_ROWS):
            rows = slice(r * PEER_NKEYS, (r + 1) * PEER_NKEYS)
            for c in range(LANE_CHUNKS):
                lanes = slice(c * LANES, (c + 1) * LANES)
                dst_ref[rows, lanes] = _gelu(a[rows, lanes]).astype(BF16) * dst_ref[rows, lanes]

    i0 = pl.ds(pl.multiple_of(e * EXPERT_ROWS, EXPERT_ROWS), EXPERT_ROWS)
    gate_tile(h0_ref, 0, i0)
    a0 = _dot(u_ref[:HALF_TILE, :], hn_ref[...])
    gate_tile(h1_ref, HALF_ROWS, i0)
    a1 = _dot(u_ref[HALF_TILE:, :], hn_ref[...])
    gelu_tile(h0_ref, a0)
    gelu_tile(h1_ref, a1)
    acc_ref[...] += (_dot(vt_ref[:, :HALF_TILE], h0_ref[...])
                     + _dot(vt_ref[:, HALF_TILE:], h1_ref[...]))

    @pl.when(e == pl.num_programs(2) - 1)
    def _():
        y = x_ref[...] + acc_ref[...]
        if final_norm:
            y = _rmsnorm_t(y, gfin_ref[...])
        o_ref[...] = y


def _peer_layer(xt, g_lanes, wq_t, subkeys, u_bf, vt_bf, gfin_lanes, final_norm):
    batch, _, seq = xt.shape
    n_steps = PEER_EXPERTS // EXPERT_TILE
    tile = pl.BlockSpec((None, D_MODEL, TOKEN_TILE), lambda b, s, e: (b, 0, s))
    full = lambda shape: pl.BlockSpec(shape, lambda b, s, e: (0,) * len(shape))
    sel = lambda dtype: pltpu.VMEM((PEER_HEADS, PEER_NKEYS, TOKEN_TILE), dtype)
    return pl.pallas_call(
        functools.partial(_peer_kernel, final_norm=final_norm),
        grid=(batch, seq // TOKEN_TILE, n_steps),
        in_specs=[
            tile,
            full((D_MODEL, LANES)),
            full((PEER_HEADS * PEER_QDIM, D_MODEL)),
            full((PEER_HEADS, 2, PEER_NKEYS, PEER_NKEYS)),
            pl.BlockSpec((EXPERT_TILE, D_MODEL), lambda b, s, e: (e, 0)),
            pl.BlockSpec((D_MODEL, EXPERT_TILE), lambda b, s, e: (0, e)),
            full((D_MODEL, LANES)),
        ],
        out_specs=tile,
        out_shape=jax.ShapeDtypeStruct(xt.shape, F32),
        scratch_shapes=[
            pltpu.VMEM((D_MODEL, TOKEN_TILE), BF16),
            pltpu.VMEM((D_MODEL, TOKEN_TILE), F32),
            pltpu.VMEM((2, PEER_NKEYS, TOKEN_TILE), F32),
            sel(F32), sel(F32), sel(BF16), sel(BF16),
            pltpu.VMEM((PEER_TOPK, LANES), F32),
            pltpu.VMEM((PEER_TOPK, LANES), F32),
            pltpu.VMEM((STAIR_PAD, LANES), F32),
            pltpu.VMEM((STAIR_PAD, LANES), F32),
            pltpu.VMEM((HALF_TILE, TOKEN_TILE), BF16),
            pltpu.VMEM((HALF_TILE, TOKEN_TILE), BF16),
        ],
        compiler_params=pltpu.CompilerParams(
            dimension_semantics=("arbitrary", "arbitrary", "arbitrary"),
            vmem_limit_bytes=VMEM_LIMIT),
        name="peer_final" if final_norm else "peer",
    )(xt, g_lanes, wq_t, subkeys, u_bf, vt_bf, gfin_lanes)


def _shared_kv_kernel(x_ref, g_ref, wkv_ref, cos_ref, sin_ref, k_ref, vt_ref, kmean_ref):
    h = _rmsnorm_t(x_ref[...], g_ref[...]).astype(BF16)
    kv = _dot(wkv_ref[...], h)
    k_t = _rope_t(kv[:MAIN_WIDTH], cos_ref[...], sin_ref[...])
    v_t = kv[MAIN_WIDTH:]
    k = k_t.T
    for jb in range(TOKEN_TILE // MOBA_BLOCK):
        toks = slice(jb * MOBA_BLOCK, (jb + 1) * MOBA_BLOCK)
        for hh in range(MOBA_HEADS):
            cols = slice(hh * HEAD_DIM, (hh + 1) * HEAD_DIM)
            kb = k[toks, cols]
            k_ref[hh, jb] = kb.astype(BF16)
            vt_ref[hh, jb] = v_t[cols, toks].astype(BF16)
            kmean_ref[hh, jb:jb + 1, :] = jnp.sum(kb, axis=0, keepdims=True) * (1.0 / MOBA_BLOCK)


def _shared_kv(xt, g_lanes, wkv_t, cos_t, sin_t):
    batch, _, seq = xt.shape
    nb = seq // MOBA_BLOCK
    per_tile = TOKEN_TILE // MOBA_BLOCK
    full = lambda shape: pl.BlockSpec(shape, lambda b, s: (0,) * len(shape))
    outs = pl.pallas_call(
        _shared_kv_kernel,
        grid=(batch, seq // TOKEN_TILE),
        in_specs=[
            pl.BlockSpec((None, D_MODEL, TOKEN_TILE), lambda b, s: (b, 0, s)),
            full((D_MODEL, LANES)),
            full((2 * MAIN_WIDTH, D_MODEL)),
            pl.BlockSpec((HALF_HEAD, TOKEN_TILE), lambda b, s: (0, s)),
            pl.BlockSpec((HALF_HEAD, TOKEN_TILE), lambda b, s: (0, s)),
        ],
        out_specs=[
            pl.BlockSpec((None, MOBA_HEADS, per_tile, MOBA_BLOCK, HEAD_DIM), lambda b, s: (b, 0, s, 0, 0)),
            pl.BlockSpec((None, MOBA_HEADS, per_tile, HEAD_DIM, MOBA_BLOCK), lambda b, s: (b, 0, s, 0, 0)),
            pl.BlockSpec((None, None, MOBA_HEADS, per_tile, HEAD_DIM), lambda b, s: (b, s, 0, 0, 0)),
        ],
        out_shape=[
            jax.ShapeDtypeStruct((batch, MOBA_HEADS, nb, MOBA_BLOCK, HEAD_DIM), BF16),
            jax.ShapeDtypeStruct((batch, MOBA_HEADS, nb, HEAD_DIM, MOBA_BLOCK), BF16),
            jax.ShapeDtypeStruct((batch, seq // TOKEN_TILE, MOBA_HEADS, per_tile, HEAD_DIM), F32),
        ],
        compiler_params=pltpu.CompilerParams(
            dimension_semantics=("arbitrary", "arbitrary"), vmem_limit_bytes=VMEM_LIMIT),
        name="shared_kv",
    )(xt, g_lanes, wkv_t, cos_t, sin_t)
    k_blocks, vt_blocks, kmean = outs
    kmean = jnp.swapaxes(kmean, 1, 2).reshape(batch, MOBA_HEADS, nb, HEAD_DIM)
    return k_blocks, vt_blocks, kmean


def _moba_front_kernel(x_ref, g_ref, win_ref, cos_ref, sin_ref, memk_ref, memvt_ref,
                       q_ref, memout_ref):
    h = _rmsnorm_t(x_ref[...], g_ref[...]).astype(BF16)
    proj = _dot(win_ref[...], h)
    q_ref[...] = _rope_t(proj[:MAIN_WIDTH], cos_ref[...], sin_ref[...])
    memout_ref[...] = _memory_attention_t(proj[MAIN_WIDTH:], memk_ref, memvt_ref).astype(BF16)


def _moba_front(xt, g_lanes, win_t, cos_t, sin_t, mem_k, mem_vt):
    batch, _, seq = xt.shape
    full = lambda shape: pl.BlockSpec(shape, lambda b, s: (0,) * len(shape))
    return pl.pallas_call(
        _moba_front_kernel,
        grid=(batch, seq // TOKEN_TILE),
        in_specs=[
            pl.BlockSpec((None, D_MODEL, TOKEN_TILE), lambda b, s: (b, 0, s)),
            full((D_MODEL, LANES)),
            full((MIX_WIDTH, D_MODEL)),
            pl.BlockSpec((HALF_HEAD, TOKEN_TILE), lambda b, s: (0, s)),
            pl.BlockSpec((HALF_HEAD, TOKEN_TILE), lambda b, s: (0, s)),
            pl.BlockSpec((None, MEM_TOKENS, MEM_WIDTH), lambda b, s: (b, 0, 0)),
            pl.BlockSpec((None, MEM_WIDTH, MEM_TOKENS), lambda b, s: (b, 0, 0)),
        ],
        out_specs=[
            pl.BlockSpec((None, MAIN_WIDTH, TOKEN_TILE), lambda b, s: (b, 0, s)),
            pl.BlockSpec((None, MEM_WIDTH, TOKEN_TILE), lambda b, s: (b, 0, s)),
        ],
        out_shape=[
            jax.ShapeDtypeStruct((batch, MAIN_WIDTH, seq), F32),
            jax.ShapeDtypeStruct((batch, MEM_WIDTH, seq), BF16),
        ],
        compiler_params=pltpu.CompilerParams(
            dimension_semantics=("arbitrary", "arbitrary"), vmem_limit_bytes=VMEM_LIMIT),
        name="moba_front",
    )(xt, g_lanes, win_t, cos_t, sin_t, mem_k, mem_vt)


def _moba_kernel(q_ref, k_ref, vt_ref, kmean_ref, o_ref, sel_ref, acc_ref):
    n = pl.program_id(1)
    nb = kmean_ref.shape[1]
    heads = range(MOBA_HEADS)
    kpos = lax.broadcasted_iota(jnp.int32, (MOBA_BLOCK, MOBA_BLOCK), 0)
    qpos = lax.broadcasted_iota(jnp.int32, (MOBA_BLOCK, MOBA_BLOCK), 1)
    causal = kpos <= qpos
    blk = lax.broadcasted_iota(jnp.int32, (nb, MOBA_BLOCK), 0)
    past = blk < n

    qs = [q_ref[h * HEAD_DIM:(h + 1) * HEAD_DIM, :] for h in heads]
    qbs = [(q * ATTN_SCALE).astype(BF16) for q in qs]
    s_own = [jnp.where(causal, _dot(k_ref[h, n], qbs[h]), NEG) for h in heads]
    gates = [jnp.where(past, _dot_hi(kmean_ref[h], qs[h]), NEG) for h in heads]
    for h in heads:
        gate = gates[h]
        rank = jnp.zeros(gate.shape, F32)
        for m in range(nb):
            gm = gate[m:m + 1, :]
            ahead = jnp.where(gm > gate, 1.0, jnp.where((gm == gate) & (blk > m), 1.0, 0.0))
            rank = rank + ahead
        sel = jnp.where((rank < MOBA_TOPK) & past, 1.0, 0.0)
        for m in range(nb):
            sel_ref[h, m] = jnp.broadcast_to(sel[m:m + 1, :], sel_ref.shape[2:])
    m0s = [jnp.max(s_own[h], axis=0, keepdims=True) for h in heads]
    p0 = [jnp.exp(s_own[h] - m0s[h]) for h in heads]
    l0s = [jnp.sum(p0[h], axis=0, keepdims=True) for h in heads]
    pv0 = [_dot(vt_ref[h, n], p0[h].astype(BF16)) for h in heads]
    for h in heads:
        acc_ref[h] = pv0[h]

    def body(j, carry):
        ms, ls = carry
        s = [_dot(k_ref[h, j], qbs[h]) for h in heads]
        picked = [sel_ref[h, j][0:1, :] > 0.0 for h in heads]
        m_new = [jnp.maximum(ms[h], jnp.where(picked[h], jnp.max(s[h], axis=0, keepdims=True), NEG))
                 for h in heads]
        alpha = [jnp.exp(ms[h] - m_new[h]) for h in heads]
        p = [jnp.exp(s[h] - jnp.where(picked[h], m_new[h], jnp.inf)) for h in heads]
        new_ls = [alpha[h] * ls[h] + jnp.sum(p[h], axis=0, keepdims=True) for h in heads]
        pv = [_dot(vt_ref[h, j], p[h].astype(BF16)) for h in heads]
        for h in heads:
            acc_ref[h] = alpha[h] * acc_ref[h] + pv[h]
        return tuple(m_new), tuple(new_ls)

    _, l_fin = lax.fori_loop(0, n, body, (tuple(m0s), tuple(l0s)))
    for h in heads:
        o_ref[h * HEAD_DIM:(h + 1) * HEAD_DIM, :] = (acc_ref[h] * (1.0 / l_fin[h])).astype(BF16)


def _moba_attention(q_t, k_blocks, vt_blocks, kmean):
    batch, _, seq = q_t.shape
    nb = seq // MOBA_BLOCK
    return pl.pallas_call(
        _moba_kernel,
        grid=(batch, nb),
        in_specs=[
            pl.BlockSpec((None, MAIN_WIDTH, MOBA_BLOCK), lambda b, n: (b, 0, n)),
            pl.BlockSpec((None, MOBA_HEADS, nb, MOBA_BLOCK, HEAD_DIM), lambda b, n: (b, 0, 0, 0, 0)),
            pl.BlockSpec((None, MOBA_HEADS, nb, HEAD_DIM, MOBA_BLOCK), lambda b, n: (b, 0, 0, 0, 0)),
            pl.BlockSpec((None, MOBA_HEADS, nb, HEAD_DIM), lambda b, n: (b, 0, 0, 0)),
        ],
        out_specs=pl.BlockSpec((None, MAIN_WIDTH, MOBA_BLOCK), lambda b, n: (b, 0, n)),
        out_shape=jax.ShapeDtypeStruct((batch, MAIN_WIDTH, seq), BF16),
        scratch_shapes=[
            pltpu.VMEM((MOBA_HEADS, nb, 8, MOBA_BLOCK), F32),
            pltpu.VMEM((MOBA_HEADS, HEAD_DIM, MOBA_BLOCK), F32),
        ],
        compiler_params=pltpu.CompilerParams(
            dimension_semantics=("arbitrary", "arbitrary"), vmem_limit_bytes=VMEM_LIMIT),
        name="moba_attention",
    )(q_t, k_blocks, vt_blocks, kmean)


def _out_proj_kernel(x_ref, attn_ref, memout_ref, wout_ref, o_ref):
    cat = jnp.concatenate([attn_ref[...], memout_ref[...]], axis=0)
    o_ref[...] = x_ref[...] + _dot(wout_ref[...], cat)


def _out_proj(xt, attn_t, memout_t, wout_t):
    batch, _, seq = xt.shape
    tile = lambda rows: pl.BlockSpec((None, rows, TOKEN_TILE), lambda b, s: (b, 0, s))
    return pl.pallas_call(
        _out_proj_kernel,
        grid=(batch, seq // TOKEN_TILE),
        in_specs=[tile(D_MODEL), tile(MAIN_WIDTH), tile(MEM_WIDTH),
                  pl.BlockSpec((D_MODEL, MIX_WIDTH), lambda b, s: (0, 0))],
        out_specs=tile(D_MODEL),
        out_shape=jax.ShapeDtypeStruct(xt.shape, F32),
        compiler_params=pltpu.CompilerParams(
            dimension_semantics=("arbitrary", "arbitrary"), vmem_limit_bytes=VMEM_LIMIT),
        name="out_proj",
    )(xt, attn_t, memout_t, wout_t)


def _lanes(v):
    return jnp.broadcast_to(v.astype(F32)[:, None], (v.shape[0], LANES))


def _rope_tables_t(seq):
    inv = 1.0 / (ROPE_THETA ** (jnp.arange(0, HEAD_DIM, 2, dtype=F32) / HEAD_DIM))
    ang = jnp.arange(seq, dtype=F32)[:, None] * inv[None, :]
    return jnp.cos(ang).T, jnp.sin(ang).T


def kernel(x, mem, norm_mix_g, w_in, w_out, w_mem_kv, pool_w, pool_scale, kv_norm_g,
           w_kv_shared, norm_ffn_g, peer_wq, peer_subkeys, peer_u, peer_v, final_norm_g):
    seq = x.shape[1]
    cos_t, sin_t = _rope_tables_t(seq)

    win_t = jnp.swapaxes(w_in, 1, 2).astype(BF16)
    wout_t = jnp.swapaxes(w_out, 1, 2).astype(BF16)
    wq_t = jnp.swapaxes(peer_wq, 1, 2).astype(BF16)
    u_bf = peer_u.astype(BF16)
    vt_bf = jnp.swapaxes(peer_v, 1, 2).astype(BF16)
    wkv_t = w_kv_shared.T.astype(BF16)
    poolw_t = jnp.swapaxes(pool_w[0], 1, 2).astype(BF16)
    wk_mem = w_mem_kv[:, :, :MEM_WIDTH].astype(BF16)
    wvt_mem = jnp.swapaxes(w_mem_kv[:, :, MEM_WIDTH:], 1, 2).astype(BF16)

    xt = jnp.swapaxes(x, 1, 2)
    mem_bf = mem.astype(BF16)
    mem_k, mem_vt = _mem_kv(mem_bf, jnp.swapaxes(mem_bf, 1, 2), wk_mem, wvt_mem)

    xt = _pool_layer(xt, _lanes(norm_mix_g[0]), win_t[0], poolw_t, _lanes(pool_scale[0]),
                     mem_k[0], mem_vt[0], wout_t[0])
    xt = _peer_layer(xt, _lanes(norm_ffn_g[0]), wq_t[0], peer_subkeys[0], u_bf[0], vt_bf[0],
                     _lanes(final_norm_g), final_norm=False)

    k_blocks, vt_blocks, kmean = _shared_kv(xt, _lanes(kv_norm_g), wkv_t, cos_t, sin_t)

    q_t, memout_t = _moba_front(xt, _lanes(norm_mix_g[1]), win_t[1], cos_t, sin_t,
                                mem_k[1], mem_vt[1])
    attn_t = _moba_attention(q_t, k_blocks, vt_blocks, kmean)
    xt = _out_proj(xt, attn_t, memout_t, wout_t[1])
    yt = _peer_layer(xt, _lanes(norm_ffn_g[1]), wq_t[1], peer_subkeys[1], u_bf[1], vt_bf[1],
                     _lanes(final_norm_g), final_norm=True)
    return jnp.swapaxes(yt, 1, 2)
```
